```python
import jax, jax.numpy as jnp
from jax import lax
import numpy as np

D_MODEL = 1024
BATCH = 32
SEQ = 2048
DEPTH = 1
DEC_BATCH = 128
DEC_SEQ = 8
PAST_LEN = 8192
PAGE_SIZE = 128

SB_HEAD_DIM = 64
SB_HEADS = (D_MODEL // 2) // SB_HEAD_DIM
SB_WIDTH = SB_HEADS * SB_HEAD_DIM
SB_LOGIT_BIAS = -7.0
Q_BLOCK = 128
POOL_WIDTH = D_MODEL - SB_WIDTH
POOL_WINDOWS = (2, 4, 8, 16)
N_POOL_GROUPS = len(POOL_WINDOWS)
POOL_GROUP_DIM = POOL_WIDTH // N_POOL_GROUPS
POOL_STATE_LEN = max(POOL_WINDOWS) - 1
PROJ_WIDTH = 3 * SB_WIDTH + POOL_WIDTH
PEER_HEADS = 8
PEER_N_KEYS = 128
PEER_N_EXPERTS = PEER_N_KEYS * PEER_N_KEYS
PEER_TOPK = 16
PEER_QUERY_DIM = 256
PEER_HALF = PEER_QUERY_DIM // 2
PEER_BLOCK = 256
ALPHA = (2.0 * DEPTH) ** 0.25
BETA = (8.0 * DEPTH) ** -0.25
LN_EPS = 1e-5

kernel_name = "hymba_stickbreak_pool_peer_step"


def layer_norm(x, g, b):
    xf = x.astype(jnp.float32)
    mu = jnp.mean(xf, axis=-1, keepdims=True)
    var = jnp.mean(jnp.square(xf - mu), axis=-1, keepdims=True)
    return ((xf - mu) * lax.rsqrt(var + LN_EPS) * g + b).astype(x.dtype)


def split_proj(h, w_in):
    p = jnp.einsum('btd,de->bte', h, w_in)
    q, k, v, u = jnp.split(p, [SB_WIDTH, 2 * SB_WIDTH, 3 * SB_WIDTH], axis=-1)
    B, T = h.shape[:2]
    hs = (B, T, SB_HEADS, SB_HEAD_DIM)
    return q.reshape(hs), k.reshape(hs), v.reshape(hs), u


def sb_block(q, k, v, sb_bias, q_pos, k_pos):
    z = jnp.einsum('bqhd,bkhd->bhqk', q, k).astype(jnp.float32) * (SB_HEAD_DIM ** -0.5)
    z = z + sb_bias.astype(jnp.float32)[None, :, None, None]
    causal = (k_pos[None, :] < q_pos[:, None])[None, None]
    log_one_minus = jnp.where(causal, jax.nn.log_sigmoid(-z), 0.0)
    rest = lax.cumsum(log_one_minus, axis=3, reverse=True) - log_one_minus
    w = jnp.where(causal, jnp.exp(jax.nn.log_sigmoid(z) + rest), 0.0)
    return jnp.einsum('bhqk,bkhd->bqhd', w.astype(v.dtype), v)


def sb_prompt(q, k, v, sb_bias):
    B, S = q.shape[:2]
    nb = S // Q_BLOCK
    qb = q.reshape(B, nb, Q_BLOCK, SB_HEADS, SB_HEAD_DIM).transpose(1, 0, 2, 3, 4)
    pos = jnp.arange(S, dtype=jnp.int32)
    qpos = pos.reshape(nb, Q_BLOCK)
    out = lax.map(lambda a: sb_block(a[0], k, v, sb_bias, a[1], pos), (qb, qpos))
    return out.transpose(1, 0, 2, 3, 4).reshape(B, S, SB_WIDTH)


def sb_sample(q, k_new, v_new, k_past, v_past, sb_bias):
    B, T = q.shape[:2]
    past_len = k_past.shape[1]
    k_all = jnp.concatenate([k_past.astype(k_new.dtype), k_new], axis=1)
    v_all = jnp.concatenate([v_past.astype(v_new.dtype), v_new], axis=1)
    k_pos = jnp.arange(past_len + T, dtype=jnp.int32)
    q_pos = past_len + jnp.arange(T, dtype=jnp.int32)
    return sb_block(q, k_all, v_all, sb_bias, q_pos, k_pos).reshape(B, T, SB_WIDTH)


def pool_mix(u_ext, n_prefix, start_pos, w_pool, pool_scale):
    B, L, C = u_ext.shape
    T = L - n_prefix
    uf = u_ext.astype(jnp.float32)
    cs = jnp.concatenate([jnp.zeros((B, 1, C), jnp.float32), jnp.cumsum(uf, axis=1)], axis=1)
    i = n_prefix + jnp.arange(T, dtype=jnp.int32)
    abs_pos = start_pos + jnp.arange(T, dtype=jnp.int32)
    hi = cs[:, i + 1]
    groups = []
    for g, w in enumerate(POOL_WINDOWS):
        sl = slice(g * POOL_GROUP_DIM, (g + 1) * POOL_GROUP_DIM)
        lo = cs[:, jnp.maximum(i + 1 - w, 0), sl]
        cnt = jnp.minimum(w, abs_pos + 1).astype(jnp.float32)[None, :, None]
        groups.append((hi[..., sl] - lo) / cnt)
    pooled = jnp.concatenate(groups, axis=-1) - uf[:, n_prefix:]
    pooled = pooled.reshape(B, T, N_POOL_GROUPS, POOL_GROUP_DIM).astype(w_pool.dtype)
    mixed = jnp.einsum('btgc,gcd->btgd', pooled, w_pool).reshape(B, T, C)
    return (mixed * pool_scale).astype(u_ext.dtype)


def peer_block(xt, w_query, sub_keys, u_table, v_table):
    N = xt.shape[0]
    q = (xt @ w_query).reshape(N, PEER_HEADS, 2, PEER_HALF)
    s = jnp.einsum('nhpc,hpkc->nhpk', q, sub_keys).astype(jnp.float32)
    sv, si = lax.top_k(s, PEER_TOPK)
    cand = sv[:, :, 0, :, None] + sv[:, :, 1, None, :]
    cid = si[:, :, 0, :, None] * PEER_N_KEYS + si[:, :, 1, None, :]
    cv, cpos = lax.top_k(cand.reshape(N, PEER_HEADS, PEER_TOPK * PEER_TOPK), PEER_TOPK)
    eid = jnp.take_along_axis(cid.reshape(N, PEER_HEADS, PEER_TOPK * PEER_TOPK), cpos, axis=-1)
    gate = jax.nn.softmax(cv, axis=-1)
    eid = eid.reshape(N, PEER_HEADS * PEER_TOPK)
    gate = gate.reshape(N, PEER_HEADS * PEER_TOPK)
    u = u_table[eid]
    act = jax.nn.gelu(jnp.einsum('ned,nd->ne', u, xt).astype(jnp.float32), approximate=False)
    coef = (gate * act).astype(v_table.dtype)
    return jnp.einsum('ne,ned->nd', coef, v_table[eid])


def peer_ffn(x, w_query, sub_keys, u_table, v_table):
    B, T, D = x.shape
    n = B * T
    nb = -(-n // PEER_BLOCK)
    pad = nb * PEER_BLOCK - n
    xt = jnp.pad(x.reshape(n, D), ((0, pad), (0, 0))).reshape(nb, PEER_BLOCK, D)
    out = lax.map(lambda xb: peer_block(xb, w_query, sub_keys, u_table, v_table), xt)
    return out.reshape(nb * PEER_BLOCK, D)[:n].reshape(B, T, D).astype(x.dtype)


def post_mix(h, attn, pooled, w_out, ln1_g, ln1_b, w_query, sub_keys, u_table, v_table, ln2_g, ln2_b):
    mix = jnp.einsum('bte,ed->btd', jnp.concatenate([attn, pooled], axis=-1), w_out)
    x1 = layer_norm(ALPHA * h + mix, ln1_g, ln1_b)
    return layer_norm(ALPHA * x1 + peer_ffn(x1, w_query, sub_keys, u_table, v_table), ln2_g, ln2_b)


def setup_inputs(seed: int = 0) -> dict:
    key = jax.random.key(seed)
    ks = jax.random.split(key, 20)
    n_pages = PAST_LEN // PAGE_SIZE
    n_used = DEC_BATCH * n_pages
    n_pool = n_used + max(1, n_used // 4)
    page_table = jax.random.permutation(ks[0], n_pool)[:n_used].reshape(DEC_BATCH, n_pages).astype(jnp.int32)
    f32 = jnp.float32
    col_scale = jnp.concatenate([jnp.ones((2 * SB_WIDTH,), f32), jnp.full((SB_WIDTH,), BETA, f32),
                                 jnp.ones((POOL_WIDTH,), f32)])
    return {
        "x_prompt": jax.random.normal(ks[1], (BATCH, SEQ, D_MODEL), f32),
        "x_sample": jax.random.normal(ks[2], (DEC_BATCH, DEC_SEQ, D_MODEL), f32),
        "cache_k": jax.random.normal(ks[3], (DEPTH, n_pool, PAGE_SIZE, SB_HEADS, SB_HEAD_DIM), f32),
        "cache_v": BETA * jax.random.normal(ks[4], (DEPTH, n_pool, PAGE_SIZE, SB_HEADS, SB_HEAD_DIM), f32),
        "state_pool": jax.random.normal(ks[5], (DEPTH, DEC_BATCH, POOL_STATE_LEN, POOL_WIDTH), f32),
        "page_table": page_table,
        "w_in": jax.random.normal(ks[6], (DEPTH, D_MODEL, PROJ_WIDTH), f32) * (D_MODEL ** -0.5) * col_scale,
        "sb_bias": SB_LOGIT_BIAS + 0.1 * jax.random.normal(ks[18], (DEPTH, SB_HEADS), f32),
        "w_out": jax.random.normal(ks[7], (DEPTH, D_MODEL, D_MODEL), f32) * (D_MODEL ** -0.5) * BETA,
        "w_pool": jax.random.normal(ks[8], (DEPTH, N_POOL_GROUPS, POOL_GROUP_DIM, POOL_GROUP_DIM), f32) * (POOL_GROUP_DIM ** -0.5),
        "pool_scale": 1.0 + 0.1 * jax.random.normal(ks[9], (DEPTH, POOL_WIDTH), f32),
        "ln1_g": 1.0 + 0.1 * jax.random.normal(ks[10], (DEPTH, D_MODEL), f32),
        "ln1_b": 0.02 * jax.random.normal(ks[11], (DEPTH, D_MODEL), f32),
        "w_query": jax.random.normal(ks[12], (DEPTH, D_MODEL, PEER_HEADS * PEER_QUERY_DIM), f32) * (D_MODEL ** -0.5),
        "sub_keys": jax.random.normal(ks[13], (DEPTH, PEER_HEADS, 2, PEER_N_KEYS, PEER_HALF), f32) * (PEER_HALF ** -0.5),
        "u_table": jax.random.normal(ks[14], (DEPTH, PEER_N_EXPERTS, D_MODEL), f32) * (D_MODEL ** -0.5),
        "v_table": jax.random.normal(ks[15], (DEPTH, PEER_N_EXPERTS, D_MODEL), f32) * BETA,
        "ln2_g": 1.0 + 0.1 * jax.random.normal(ks[16], (DEPTH, D_MODEL), f32),
        "ln2_b": 0.02 * jax.random.normal(ks[17], (DEPTH, D_MODEL), f32),
    }


def reference(x_prompt, x_sample, cache_k, cache_v, state_pool, page_table, w_in, sb_bias, w_out, w_pool,
              pool_scale, ln1_g, ln1_b, w_query, sub_keys, u_table, v_table, ln2_g, ln2_b):
    hp, hs = x_prompt, x_sample
    dec_b, n_pages = page_table.shape
    past_len = n_pages * PAGE_SIZE
    kp_new, vp_new, pp_new, ks_new, vs_new, ps_new = [], [], [], [], [], []
    for l in range(DEPTH):
        q, k, v, u = split_proj(hp, w_in[l])
        attn = sb_prompt(q, k, v, sb_bias[l])
        pooled = pool_mix(u, 0, 0, w_pool[l], pool_scale[l])
        kp_new.append(k)
        vp_new.append(v)
        pp_new.append(u[:, -POOL_STATE_LEN:])
        hp = post_mix(hp, attn, pooled, w_out[l], ln1_g[l], ln1_b[l], w_query[l], sub_keys[l],
                      u_table[l], v_table[l], ln2_g[l], ln2_b[l])
        qs, kss, vss, us = split_proj(hs, w_in[l])
        k_past = cache_k[l][page_table].reshape(dec_b, past_len, SB_HEADS, SB_HEAD_DIM)
        v_past = cache_v[l][page_table].reshape(dec_b, past_len, SB_HEADS, SB_HEAD_DIM)
        attn_s = sb_sample(qs, kss, vss, k_past, v_past, sb_bias[l])
        u_ext = jnp.concatenate([state_pool[l].astype(us.dtype), us], axis=1)
        pooled_s = pool_mix(u_ext, POOL_STATE_LEN, past_len, w_pool[l], pool_scale[l])
        ks_new.append(kss)
        vs_new.append(vss)
        ps_new.append(u_ext[:, -POOL_STATE_LEN:])
        hs = post_mix(hs, attn_s, pooled_s, w_out[l], ln1_g[l], ln1_b[l], w_query[l], sub_keys[l],
                      u_table[l], v_table[l], ln2_g[l], ln2_b[l])
    return (hp, hs, jnp.stack(kp_new), jnp.stack(vp_new), jnp.stack(pp_new),
            jnp.stack(ks_new), jnp.stack(vs_new), jnp.stack(ps_new))
```

```python
import functools
import math

import jax
import jax.numpy as jnp
from jax import lax
from jax.experimental import pallas as pl
from jax.experimental.pallas import tpu as pltpu

F32 = jnp.float32
BF16 = jnp.bfloat16

SB_HEADS = 8
SB_HEAD_DIM = 64
SB_WIDTH = SB_HEADS * SB_HEAD_DIM
POOL_WINDOWS = (2, 4, 8, 16)
POOL_GROUP_DIM = 128
POOL_WIDTH = len(POOL_WINDOWS) * POOL_GROUP_DIM
POOL_STATE_LEN = max(POOL_WINDOWS) - 1
POOL_PREV_ROWS = 16
PAGE_SIZE = 128
PEER_HEADS = 8
PEER_N_KEYS = 128
PEER_TOPK = 16
PEER_HALF = 128
LN_EPS = 1e-5

LANES = 128
SUBLANES = 8
VMEM_LIMIT_BYTES = 56 * 1024 * 1024

NEG_BIG = -1e30


def _compiler_params(semantics):
    return pltpu.CompilerParams(dimension_semantics=semantics, vmem_limit_bytes=VMEM_LIMIT_BYTES)


def _proj_kernel(x_ref, w_ref, q_ref, k_ref, v_ref, u_ref, kb_ref, vb_ref):
    xb = x_ref[...].astype(BF16)
    w = SB_WIDTH
    q = jnp.dot(xb, w_ref[:, 0:w], preferred_element_type=F32)
    q_ref[...] = (q * (SB_HEAD_DIM ** -0.5)).astype(BF16)
    k = jnp.dot(xb, w_ref[:, w:2 * w], preferred_element_type=F32)
    k_ref[...] = k
    kb_ref[...] = k.astype(BF16)
    v = jnp.dot(xb, w_ref[:, 2 * w:3 * w], preferred_element_type=F32)
    v_ref[...] = v
    vb_ref[...] = v.astype(BF16)
    u_ref[...] = jnp.dot(xb, w_ref[:, 3 * w:], preferred_element_type=F32)


def _in_proj(x2d, w_in_bf, tm):
    n, d = x2d.shape
    pw = w_in_bf.shape[1]
    row = lambda i: (i, 0)
    out_block = pl.BlockSpec((tm, SB_WIDTH), row)
    return pl.pallas_call(
        _proj_kernel,
        grid=(n // tm,),
        in_specs=[pl.BlockSpec((tm, d), row), pl.BlockSpec((d, pw), lambda i: (0, 0))],
        out_specs=[out_block] * 6,
        out_shape=[jax.ShapeDtypeStruct((n, SB_WIDTH), BF16),
                   jax.ShapeDtypeStruct((n, SB_WIDTH), F32),
                   jax.ShapeDtypeStruct((n, SB_WIDTH), F32),
                   jax.ShapeDtypeStruct((n, POOL_WIDTH), F32),
                   jax.ShapeDtypeStruct((n, SB_WIDTH), BF16),
                   jax.ShapeDtypeStruct((n, SB_WIDTH), BF16)],
        compiler_params=_compiler_params(("parallel",)),
        name="in_proj",
    )(x2d, w_in_bf)


def _suffix_sum_matrix(tk):
    r = lax.broadcasted_iota(jnp.int32, (tk, 2 * tk), 0)
    c = lax.broadcasted_iota(jnp.int32, (tk, 2 * tk), 1)
    return jnp.where((r > c) | (c >= tk), 1.0, 0.0).astype(BF16)


def _sb_tile(z, carry, tmat, causal):
    tk = z.shape[1]
    log1m = jnp.minimum(-z, 0.0) - jnp.log(1.0 + jnp.exp(-jnp.abs(z)))
    if causal is not None:
        log1m = jnp.where(causal, log1m, 0.0)
    sums = jnp.dot(log1m.astype(BF16), tmat, preferred_element_type=F32)
    w = jnp.exp((z + log1m) + (sums[:, :tk] + carry))
    if causal is not None:
        w = jnp.where(causal, w, 0.0)
    return w, carry + sums[:, tk:]


def _sb_prompt_kernel(bias_ref, q_ref, k_ref, v_ref, o_ref, *, tq):
    hp = pl.program_id(1)
    qi = pl.program_id(2)
    tmat = _suffix_sum_matrix(tq)
    lane = lax.broadcasted_iota(jnp.int32, (tq, LANES), 1)
    q = q_ref[0]
    qh = [jnp.where((lane >= h * SB_HEAD_DIM) & (lane < (h + 1) * SB_HEAD_DIM), q, jnp.zeros_like(q))
          for h in range(2)]
    bias = [bias_ref[2 * hp + h] for h in range(2)]
    row = lax.broadcasted_iota(jnp.int32, (tq, tq), 0)
    col = lax.broadcasted_iota(jnp.int32, (tq, tq), 1)
    strict = col < row

    def block(j, carries, causal):
        start = pl.multiple_of(j * tq, tq)
        kj = k_ref[0, pl.ds(start, tq), :]
        vj = v_ref[0, pl.ds(start, tq), :]
        out = []
        for h in range(2):
            c, acc = carries[h]
            z = lax.dot_general(qh[h], kj, (((1,), (1,)), ((), ())), preferred_element_type=F32) + bias[h]
            w, c = _sb_tile(z, c, tmat, causal)
            acc = acc + jnp.dot(w.astype(BF16), vj, preferred_element_type=F32)
            out.append((c, acc))
        return tuple(out)

    zero = jnp.zeros((tq, LANES), F32)
    carries = block(qi, ((zero, zero), (zero, zero)), strict)
    carries = lax.fori_loop(0, qi, lambda it, cs: block(qi - 1 - it, cs, None), carries)
    o_ref[0] = jnp.where(lane < SB_HEAD_DIM, carries[0][1], carries[1][1]).astype(o_ref.dtype)


def _sb_prompt(bias, q, kb, vb, tq):
    b, s, _ = q.shape
    assert tq == LANES and s % tq == 0
    qspec = pl.BlockSpec((1, tq, LANES), lambda bi, hp, qi: (bi, qi, hp))
    kvspec = pl.BlockSpec((1, s, LANES), lambda bi, hp, qi: (bi, 0, hp))
    return pl.pallas_call(
        functools.partial(_sb_prompt_kernel, tq=tq),
        grid=(b, SB_HEADS // 2, s // tq),
        in_specs=[pl.BlockSpec(memory_space=pltpu.SMEM), qspec, kvspec, kvspec],
        out_specs=qspec,
        out_shape=jax.ShapeDtypeStruct((b, s, SB_WIDTH), BF16),
        compiler_params=_compiler_params(("parallel", "parallel", "arbitrary")),
        name="sb_prompt",
    )(bias, q, kb, vb)


def _sb_sample_kernel(pt_ref, bias_ref, q_ref, kn_ref, vn_ref, *rest, pages_per_step, t_new):
    del pt_ref
    kpages = rest[:pages_per_step]
    vpages = rest[pages_per_step:2 * pages_per_step]
    o_ref = rest[2 * pages_per_step]
    qbd_ref, biasrow_ref, carry_ref, acc_ref = rest[2 * pages_per_step + 1:]
    g = pl.program_id(1)
    rows = SB_HEADS * t_new
    tmat = _suffix_sum_matrix(PAGE_SIZE)

    def key_block(k_bf, v_bf, causal):
        z = lax.dot_general(qbd_ref[...], k_bf, (((1,), (1,)), ((), ())), preferred_element_type=F32)
        w, c = _sb_tile(z + biasrow_ref[...], carry_ref[...], tmat, causal)
        carry_ref[...] = c
        acc_ref[...] += jnp.dot(w.astype(BF16), v_bf, preferred_element_type=F32)

    @pl.when(g == 0)
    def _():
        r = lax.broadcasted_iota(jnp.int32, (rows, SB_WIDTH), 0)
        c = lax.broadcasted_iota(jnp.int32, (rows, SB_WIDTH), 1)
        qrep = jnp.concatenate([q_ref[0].astype(F32)] * SB_HEADS, axis=0)
        qbd_ref[...] = jnp.where(c // SB_HEAD_DIM == r // t_new, qrep, 0.0).astype(BF16)
        rr = lax.broadcasted_iota(jnp.int32, (rows, PAGE_SIZE), 0)
        brow = jnp.zeros((rows, PAGE_SIZE), F32)
        for h in range(SB_HEADS):
            brow = jnp.where(rr // t_new == h, bias_ref[h], brow)
        biasrow_ref[...] = brow
        carry_ref[...] = jnp.zeros_like(carry_ref)
        acc_ref[...] = jnp.zeros_like(acc_ref)
        pad = jnp.zeros((PAGE_SIZE - t_new, SB_WIDTH), F32)
        kn = jnp.concatenate([kn_ref[0], pad], axis=0).astype(BF16)
        vn = jnp.concatenate([vn_ref[0], pad], axis=0).astype(BF16)
        cc = lax.broadcasted_iota(jnp.int32, (rows, PAGE_SIZE), 1)
        key_block(kn, vn, cc < rr % t_new)

    for p in range(pages_per_step):
        key_block(kpages[p][0].astype(BF16), vpages[p][0].astype(BF16), None)

    @pl.when(g == pl.num_programs(1) - 1)
    def _():
        c = lax.broadcasted_iota(jnp.int32, (t_new, SB_WIDTH), 1)
        out = jnp.zeros((t_new, SB_WIDTH), F32)
        for h in range(SB_HEADS):
            out = jnp.where(c // SB_HEAD_DIM == h, acc_ref[h * t_new:(h + 1) * t_new, :], out)
        o_ref[0] = out.astype(o_ref.dtype)


def _sb_sample(page_table, bias, q, k_new, v_new, cache_k, cache_v, pages_per_step):
    b, t_new, _ = q.shape
    n_pages = page_table.shape[1]
    assert n_pages % pages_per_step == 0 and t_new % SUBLANES == 0
    steps = n_pages // pages_per_step
    rows = SB_HEADS * t_new

    def page_spec(p):
        return pl.BlockSpec((1, PAGE_SIZE, SB_WIDTH),
                            lambda bi, g, pt: (pt[bi, n_pages - 1 - (g * pages_per_step + p)], 0, 0))

    tok = lambda dt: pl.BlockSpec((1, t_new, SB_WIDTH), lambda bi, g, pt: (bi, 0, 0))
    grid_spec = pltpu.PrefetchScalarGridSpec(
        num_scalar_prefetch=1,
        grid=(b, steps),
        in_specs=[pl.BlockSpec(memory_space=pltpu.SMEM), tok(BF16), tok(F32), tok(F32)]
        + [page_spec(p) for p in range(pages_per_step)] * 2,
        out_specs=pl.BlockSpec((1, t_new, SB_WIDTH), lambda bi, g, pt: (bi, 0, 0)),
        scratch_shapes=[pltpu.VMEM((rows, SB_WIDTH), BF16),
                        pltpu.VMEM((rows, PAGE_SIZE), F32),
                        pltpu.VMEM((rows, PAGE_SIZE), F32),
                        pltpu.VMEM((rows, SB_WIDTH), F32)],
    )
    return pl.pallas_call(
        functools.partial(_sb_sample_kernel, pages_per_step=pages_per_step, t_new=t_new),
        grid_spec=grid_spec,
        out_shape=jax.ShapeDtypeStruct((b, t_new, SB_WIDTH), F32),
        compiler_params=_compiler_params(("parallel", "arbitrary")),
        name="sb_sample",
    )(page_table, bias, q, k_new, v_new,
      *([cache_k] * pages_per_step), *([cache_v] * pages_per_step))


def _layer_norm(x, g, b):
    mu = jnp.mean(x, axis=-1, keepdims=True)
    xc = x - mu
    var = jnp.mean(xc * xc, axis=-1, keepdims=True)
    return xc * lax.rsqrt(var + LN_EPS) * g + b


def _mix_kernel(x_ref, attn_ref, u_ref, prev_ref, wpool_ref, pscale_ref, wout_ref, g_ref, b_ref, o_ref,
                *, alpha, pos0, zero_first_prev):
    bb, t, _ = u_ref.shape
    i = pl.program_id(1)
    prev = prev_ref[...]
    if zero_first_prev:
        prev = jnp.where(i == 0, 0.0, prev)
    ext = jnp.concatenate([prev, u_ref[...]], axis=1)
    pos = pos0 + i * t + lax.broadcasted_iota(jnp.int32, (1, t, 1), 1)
    mixed = []
    for gi, win in enumerate(POOL_WINDOWS):
        sl = slice(gi * POOL_GROUP_DIM, (gi + 1) * POOL_GROUP_DIM)
        acc = ext[:, :, sl]
        span = 1
        while span < win:
            acc = acc[:, span:, :] + acc[:, :acc.shape[1] - span, :]
            span *= 2
        tok = ext[:, POOL_PREV_ROWS:, sl]
        inv = 1.0 / jnp.minimum(win, pos + 1).astype(F32)
        pooled = acc[:, acc.shape[1] - t:, :] * inv - tok
        pooled = pooled.reshape(bb * t, POOL_GROUP_DIM).astype(BF16)
        mixed.append(jnp.dot(pooled, wpool_ref[gi], preferred_element_type=F32))
    pool_out = (jnp.concatenate(mixed, axis=-1) * pscale_ref[...]).astype(BF16)
    attn = attn_ref[...].reshape(bb * t, SB_WIDTH).astype(BF16)
    mix = (jnp.dot(attn, wout_ref[0:SB_WIDTH, :], preferred_element_type=F32)
           + jnp.dot(pool_out, wout_ref[SB_WIDTH:, :], preferred_element_type=F32))
    x = x_ref[...].reshape(bb * t, x_ref.shape[2])
    o_ref[...] = _layer_norm(alpha * x + mix, g_ref[...], b_ref[...]).reshape(o_ref.shape)


def _mix(x, attn, u, prev, wpool_bf, pscale, wout_bf, ln_g, ln_b, *, bb, t, alpha, pos0, zero_first_prev):
    b, s, d = x.shape
    assert b % bb == 0 and s % t == 0 and t % SUBLANES == 0
    if zero_first_prev:
        assert t % POOL_PREV_ROWS == 0
        prev_map = lambda bi, i: (bi, jnp.maximum(i * (t // POOL_PREV_ROWS) - 1, 0), 0)
    else:
        assert s == t
        prev_map = lambda bi, i: (bi, 0, 0)
    tile = lambda w: pl.BlockSpec((bb, t, w), lambda bi, i: (bi, i, 0))
    const2 = lambda shape: pl.BlockSpec(shape, lambda bi, i: (0, 0))
    return pl.pallas_call(
        functools.partial(_mix_kernel, alpha=alpha, pos0=pos0, zero_first_prev=zero_first_prev),
        grid=(b // bb, s // t),
        in_specs=[tile(d), tile(SB_WIDTH), tile(POOL_WIDTH),
                  pl.BlockSpec((bb, POOL_PREV_ROWS, POOL_WIDTH), prev_map),
                  pl.BlockSpec(wpool_bf.shape, lambda bi, i: (0, 0, 0)),
                  const2((1, POOL_WIDTH)), const2(wout_bf.shape), const2((1, d)), const2((1, d))],
        out_specs=tile(d),
        out_shape=jax.ShapeDtypeStruct((b, s, d), F32),
        compiler_params=_compiler_params(("parallel", "arbitrary")),
        name="mix_ln1",
    )(x, attn, u, prev, wpool_bf, pscale, wout_bf, ln_g, ln_b)


def _top_values(s, count):
    vals = []
    for r in range(count):
        m = jnp.max(s, axis=0, keepdims=True)
        vals.append(m)
        if r + 1 < count:
            s = jnp.where(s == m, NEG_BIG, s)
    return vals


def _stack_rows(vals, rows):
    tm = vals[0].shape[1]
    ridx = lax.broadcasted_iota(jnp.int32, (rows, tm), 0)
    tile = jnp.full((rows, tm), NEG_BIG, F32)
    for r, v in enumerate(vals):
        tile = jnp.where(ridx == r, v, tile)
    return tile


def _peer_route(xb, wqt_ref, keys_ref, s2_ref, cut_ref, ea_ref, eb_ref):
    n_top = PEER_TOPK + 1
    for h in range(PEER_HEADS):
        scores = []
        for p in range(2):
            hp = 2 * h + p
            qt = lax.dot_general(wqt_ref[hp * PEER_HALF:(hp + 1) * PEER_HALF, :], xb,
                                 (((1,), (1,)), ((), ())), preferred_element_type=F32)
            scores.append(jnp.dot(keys_ref[hp], qt.astype(BF16), preferred_element_type=F32))
        s1, s2 = scores
        a = _top_values(s1, n_top)
        b = _top_values(s2, n_top)
        b_full = _stack_rows(b, 3 * SUBLANES)
        b_head = b_full[0:SUBLANES, :]
        cand = jnp.concatenate([a[0] + b_full] + [a[k] + b_head for k in range(1, n_top)], axis=0)
        cv = _top_values(cand, n_top)
        cut = 0.5 * (cv[PEER_TOPK - 1] + cv[PEER_TOPK])
        z = cv[0] * 0.0 + 1.0
        for r in range(1, PEER_TOPK):
            z = z + jnp.exp(cv[r] - cv[0])
        s2_ref[h] = s2
        cut_ref[h] = cut - s1
        ea_ref[h] = jnp.exp(s1 - a[0]) / z
        eb_ref[h] = jnp.exp(s2 - b[0])


def _peer_kernel(x_ref, wqt_ref, keys_ref, u_ref, vt_ref, g_ref, b_ref, o_ref,
                 xb_ref, s2_ref, cut_ref, ea_ref, eb_ref, acc_ref, coef_ref, *, alpha, rows_per_chunk):
    c = pl.program_id(1)
    tm = x_ref.shape[0]

    @pl.when(c == 0)
    def _():
        xb_ref[...] = x_ref[...].astype(BF16)
        _peer_route(xb_ref[...], wqt_ref, keys_ref, s2_ref, cut_ref, ea_ref, eb_ref)
        acc_ref[...] = jnp.zeros_like(acc_ref)

    act = lax.dot_general(u_ref[...], xb_ref[...], (((1,), (1,)), ((), ())),
                          preferred_element_type=F32)
    i0 = pl.multiple_of(c * rows_per_chunk, rows_per_chunk)
    for ii in range(rows_per_chunk):
        gate = jnp.zeros((PEER_N_KEYS, tm), F32)
        for h in range(PEER_HEADS):
            cut_i = cut_ref[h, pl.ds(i0 + ii, 1), :]
            ea_i = ea_ref[h, pl.ds(i0 + ii, 1), :]
            gate = gate + ea_i * jnp.where(s2_ref[h] >= cut_i, eb_ref[h], 0.0)
        a = act[ii * PEER_N_KEYS:(ii + 1) * PEER_N_KEYS, :]
        gelu = 0.5 * a * (1.0 + lax.erf(a * (2.0 ** -0.5)))
        coef_ref[ii * PEER_N_KEYS:(ii + 1) * PEER_N_KEYS, :] = (gate * gelu).astype(BF16)
    acc_ref[...] += jnp.dot(vt_ref[...], coef_ref[...], preferred_element_type=F32)

    @pl.when(c == pl.num_programs(1) - 1)
    def _():
        peer = acc_ref[...].T
        o_ref[...] = _layer_norm(alpha * x_ref[...] + peer, g_ref[...], b_ref[...])


def _peer(x1, wqt_bf, keys_bf, u_bf, vt_bf, ln_g, ln_b, *, tm, rows_per_chunk, alpha):
    n, d = x1.shape
    n_exp = u_bf.shape[0]
    chunk = rows_per_chunk * PEER_N_KEYS
    assert n % tm == 0 and n_exp % chunk == 0 and rows_per_chunk % SUBLANES == 0
    const = lambda shape: pl.BlockSpec(shape, lambda i, c: (0,) * len(shape))
    route = pltpu.VMEM((PEER_HEADS, PEER_N_KEYS, tm), F32)
    return pl.pallas_call(
        functools.partial(_peer_kernel, alpha=alpha, rows_per_chunk=rows_per_chunk),
        grid=(n // tm, n_exp // chunk),
        in_specs=[pl.BlockSpec((tm, d), lambda i, c: (i, 0)),
                  const(wqt_bf.shape), const(keys_bf.shape),
                  pl.BlockSpec((chunk, d), lambda i, c: (c, 0)),
                  pl.BlockSpec((d, chunk), lambda i, c: (0, c)),
                  const((1, d)), const((1, d))],
        out_specs=pl.BlockSpec((tm, d), lambda i, c: (i, 0)),
        out_shape=jax.ShapeDtypeStruct((n, d), F32),
        scratch_shapes=[pltpu.VMEM((tm, d), BF16), route, route, route, route,
                        pltpu.VMEM((d, tm), F32), pltpu.VMEM((chunk, tm), BF16)],
        compiler_params=_compiler_params(("parallel", "arbitrary")),
        name="peer_ln2",
    )(x1, wqt_bf, keys_bf, u_bf, vt_bf, ln_g, ln_b)


def _pick(n, prefs):
    for p in prefs:
        if n % p == 0:
            return p
    return n


def kernel(x_prompt, x_sample, cache_k, cache_v, state_pool, page_table, w_in, sb_bias, w_out, w_pool,
           pool_scale, ln1_g, ln1_b, w_query, sub_keys, u_table, v_table, ln2_g, ln2_b):
    depth = w_in.shape[0]
    b, s, d = x_prompt.shape
    db, t_new, _ = x_sample.shape
    n_pages = page_table.shape[1]
    past_len = n_pages * PAGE_SIZE
    alpha = (2.0 * depth) ** 0.25
    n_pool = cache_k.shape[1]

    hp, hs = x_prompt, x_sample
    outs = [[] for _ in range(6)]
    for l in range(depth):
        w_in_bf = w_in[l].astype(BF16)
        wout_bf = w_out[l].astype(BF16)
        wpool_bf = w_pool[l].astype(BF16)
        wqt_bf = w_query[l].T.astype(BF16)
        keys_bf = sub_keys[l].reshape(2 * PEER_HEADS, PEER_N_KEYS, PEER_HALF).astype(BF16)
        u_bf = u_table[l].astype(BF16)
        vt_bf = v_table[l].T.astype(BF16)
        pscale = pool_scale[l].reshape(1, POOL_WIDTH)
        g1, b1 = ln1_g[l].reshape(1, d), ln1_b[l].reshape(1, d)
        g2, b2 = ln2_g[l].reshape(1, d), ln2_b[l].reshape(1, d)
        bias = sb_bias[l].astype(F32)
        peer = functools.partial(_peer, wqt_bf=wqt_bf, keys_bf=keys_bf, u_bf=u_bf, vt_bf=vt_bf,
                                 ln_g=g2, ln_b=b2, rows_per_chunk=SUBLANES, alpha=alpha)

        n = b * s
        q, k, v, u, kb, vb = _in_proj(hp.reshape(n, d), w_in_bf, _pick(n, (512, 256, 128)))
        attn = _sb_prompt(bias, q.reshape(b, s, SB_WIDTH), kb.reshape(b, s, SB_WIDTH),
                          vb.reshape(b, s, SB_WIDTH), LANES)
        u3 = u.reshape(b, s, POOL_WIDTH)
        x1 = _mix(hp, attn, u3, u3, wpool_bf, pscale, wout_bf, g1, b1,
                  bb=1, t=_pick(s, (512, 256, 128)), alpha=alpha, pos0=0, zero_first_prev=True)
        hp = peer(x1.reshape(n, d), tm=_pick(n, (256, 128))).reshape(b, s, d)
        outs[0].append(k.reshape(b, s, SB_HEADS, SB_HEAD_DIM))
        outs[1].append(v.reshape(b, s, SB_HEADS, SB_HEAD_DIM))
        outs[2].append(u3[:, s - POOL_STATE_LEN:])

        ns = db * t_new
        qs, ks, vs, us, _, _ = _in_proj(hs.reshape(ns, d), w_in_bf, _pick(ns, (512, 256, 128)))
        attn_s = _sb_sample(page_table, bias, qs.reshape(db, t_new, SB_WIDTH),
                            ks.reshape(db, t_new, SB_WIDTH), vs.reshape(db, t_new, SB_WIDTH),
                            cache_k[l].reshape(n_pool, PAGE_SIZE, SB_WIDTH),
                            cache_v[l].reshape(n_pool, PAGE_SIZE, SB_WIDTH),
                            _pick(n_pages, (8, 4, 2, 1)))
        us3 = us.reshape(db, t_new, POOL_WIDTH)
        state = state_pool[l].astype(F32)
        prev = jnp.pad(state, ((0, 0), (POOL_PREV_ROWS - POOL_STATE_LEN, 0), (0, 0)))
        x1s = _mix(hs, attn_s, us3, prev, wpool_bf, pscale, wout_bf, g1, b1,
                   bb=_pick(db, (16, 8, 4, 2, 1)), t=t_new, alpha=alpha, pos0=past_len, zero_first_prev=False)
        hs = peer(x1s.reshape(ns, d), tm=_pick(ns, (256, 128))).reshape(db, t_new, d)
        outs[3].append(ks.reshape(db, t_new, SB_HEADS, SB_HEAD_DIM))
        outs[4].append(vs.reshape(db, t_new, SB_HEADS, SB_HEAD_DIM))
        outs[5].append(jnp.concatenate([state, us3], axis=1)[:, -POOL_STATE_LEN:])

    return (hp, hs, jnp.stack(outs[0]), jnp.stack(outs[1]), jnp.stack(outs[2]),
            jnp.stack(outs[3]), jnp.stack(outs[4]), jnp.stack(outs[5]))
```

```python
import functools
import math

import jax
import jax.numpy as jnp
from jax import lax
from jax.experimental import pallas as pl
from jax.experimental.pallas import tpu as pltpu

F32 = jnp.float32
BF16 = jnp.bfloat16

SB_HEADS = 8
SB_HEAD_DIM = 64
SB_WIDTH = SB_HEADS * SB_HEAD_DIM
POOL_WINDOWS = (2, 4, 8, 16)
POOL_GROUP_DIM = 128
POOL_WIDTH = len(POOL_WINDOWS) * POOL_GROUP_DIM
POOL_STATE_LEN = max(POOL_WINDOWS) - 1
POOL_PREV_ROWS = 16
PAGE_SIZE = 128
PEER_HEADS = 8
PEER_N_KEYS = 128
PEER_TOPK = 16
PEER_HALF = 128
LN_EPS = 1e-5

LANES = 128
SUBLANES = 8
VMEM_LIMIT_BYTES = 56 * 1024 * 1024

NEG_BIG = -1e30
_NT = (((1,), (1,)), ((), ()))


def _compiler_params(semantics):
    return pltpu.CompilerParams(dimension_semantics=semantics, vmem_limit_bytes=VMEM_LIMIT_BYTES)


def _proj_kernel(x_ref, w_ref, q_ref, k_ref, v_ref, u_ref, kb_ref, vb_ref):
    xb = x_ref[...].astype(BF16)
    w = SB_WIDTH
    q = jnp.dot(xb, w_ref[:, 0:w], preferred_element_type=F32)
    q_ref[...] = (q * (SB_HEAD_DIM ** -0.5)).astype(BF16)
    k = jnp.dot(xb, w_ref[:, w:2 * w], preferred_element_type=F32)
    k_ref[...] = k
    kb_ref[...] = k.astype(BF16)
    v = jnp.dot(xb, w_ref[:, 2 * w:3 * w], preferred_element_type=F32)
    v_ref[...] = v
    vb_ref[...] = v.astype(BF16)
    u_ref[...] = jnp.dot(xb, w_ref[:, 3 * w:], preferred_element_type=F32)


def _in_proj(x2d, w_in_bf, tm):
    n, d = x2d.shape
    pw = w_in_bf.shape[1]
    row = lambda i: (i, 0)
    out_block = pl.BlockSpec((tm, SB_WIDTH), row)
    return pl.pallas_call(
        _proj_kernel,
        grid=(n // tm,),
        in_specs=[pl.BlockSpec((tm, d), row), pl.BlockSpec((d, pw), lambda i: (0, 0))],
        out_specs=[out_block] * 6,
        out_shape=[jax.ShapeDtypeStruct((n, SB_WIDTH), BF16),
                   jax.ShapeDtypeStruct((n, SB_WIDTH), F32),
                   jax.ShapeDtypeStruct((n, SB_WIDTH), F32),
                   jax.ShapeDtypeStruct((n, POOL_WIDTH), F32),
                   jax.ShapeDtypeStruct((n, SB_WIDTH), BF16),
                   jax.ShapeDtypeStruct((n, SB_WIDTH), BF16)],
        compiler_params=_compiler_params(("parallel",)),
        name="in_proj",
    )(x2d, w_in_bf)


def _suffix_sum_matrix(tk):
    r = lax.broadcasted_iota(jnp.int32, (tk, 2 * tk), 0)
    c = lax.broadcasted_iota(jnp.int32, (tk, 2 * tk), 1)
    return jnp.where((r > c) | (c >= tk), 1.0, 0.0).astype(BF16)


def _sb_weights(zs, carry, tmat, causal):
    rows, tk = zs[0].shape
    log1m = [jnp.minimum(-z, 0.0) - jnp.log(1.0 + jnp.exp(-jnp.abs(z))) for z in zs]
    if causal is not None:
        log1m = [jnp.where(causal, l, 0.0) for l in log1m]
    sums = jnp.dot(jnp.concatenate(log1m, axis=0).astype(BF16), tmat, preferred_element_type=F32)
    ws = []
    for p, (z, l) in enumerate(zip(zs, log1m)):
        sp = sums[p * rows:(p + 1) * rows]
        w = jnp.exp((z + l) + (sp[:, :tk] + carry))
        if causal is not None:
            w = jnp.where(causal, w, 0.0)
        ws.append(w.astype(BF16))
        carry = carry + sp[:, tk:]
    return ws, carry


def _sb_prompt_kernel(bias_ref, q_ref, k_ref, v_ref, o_ref,
                      qs_ref, zl_ref, l_ref, w_ref, carry_ref, acc_ref, *, tq):
    qi = pl.program_id(1)
    n_pairs = SB_HEADS // 2
    r = lax.broadcasted_iota(jnp.int32, (tq, tq), 0)
    c = lax.broadcasted_iota(jnp.int32, (tq, tq), 1)
    strict = c < r
    tmat = jnp.where(r > c, 1.0, 0.0).astype(BF16)
    lane = lax.broadcasted_iota(jnp.int32, (tq, LANES), 1)
    low = lane < SB_HEAD_DIM
    for hp in range(n_pairs):
        q = q_ref[0, :, hp * LANES:(hp + 1) * LANES]
        zero = jnp.zeros_like(q)
        qs_ref[hp, 0:tq, :] = jnp.where(low, q, zero)
        qs_ref[hp, tq:2 * tq, :] = jnp.where(low, zero, q)
    carry_ref[...] = jnp.zeros_like(carry_ref)
    acc_ref[...] = jnp.zeros_like(acc_ref)

    def block(j, causal):
        start = pl.multiple_of(j * tq, tq)
        for hp in range(n_pairs):
            kj = k_ref[0, pl.ds(start, tq), hp * LANES:(hp + 1) * LANES]
            z2 = lax.dot_general(qs_ref[hp], kj, (((1,), (1,)), ((), ())), preferred_element_type=F32)
            for h2 in range(2):
                h = 2 * hp + h2
                z = z2[h2 * tq:(h2 + 1) * tq, :] + bias_ref[h]
                log1m = jnp.minimum(-z, 0.0) - jnp.log(1.0 + jnp.exp(-jnp.abs(z)))
                if causal:
                    log1m = jnp.where(strict, log1m, 0.0)
                zl_ref[h] = z + log1m
                l_ref[h * tq:(h + 1) * tq, :] = log1m.astype(BF16)
        later = jnp.dot(l_ref[...], tmat, preferred_element_type=F32)
        for hp in range(n_pairs):
            for h2 in range(2):
                h = 2 * hp + h2
                rest = later[h * tq:(h + 1) * tq, :]
                first = l_ref[h * tq:(h + 1) * tq, 0:1].astype(F32)
                rowsum = jnp.broadcast_to(rest[:, 0:1] + first, (tq, LANES))
                carry = carry_ref[h]
                w = jnp.exp(zl_ref[h] + rest + jnp.concatenate([carry] * (tq // LANES), axis=1))
                if causal:
                    w = jnp.where(strict, w, 0.0)
                w_ref[hp, h2 * tq:(h2 + 1) * tq, :] = w.astype(BF16)
                carry_ref[h] = carry + rowsum
        for hp in range(n_pairs):
            vj = v_ref[0, pl.ds(start, tq), hp * LANES:(hp + 1) * LANES]
            acc_ref[hp] += jnp.dot(w_ref[hp], vj, preferred_element_type=F32)

    block(qi, True)

    def body(it, _):
        block(qi - 1 - it, False)
        return 0

    lax.fori_loop(0, qi, body, 0)
    for hp in range(n_pairs):
        o_ref[0, :, hp * LANES:(hp + 1) * LANES] = jnp.where(
            low, acc_ref[hp, 0:tq, :], acc_ref[hp, tq:2 * tq, :]).astype(o_ref.dtype)


def _sb_prompt(bias, q, kb, vb, tq):
    b, s, _ = q.shape
    assert tq % LANES == 0 and s % tq == 0
    qspec = pl.BlockSpec((1, tq, SB_WIDTH), lambda bi, qi: (bi, qi, 0))
    kvspec = pl.BlockSpec((1, s, SB_WIDTH), lambda bi, qi: (bi, 0, 0))
    n_pairs = SB_HEADS // 2
    return pl.pallas_call(
        functools.partial(_sb_prompt_kernel, tq=tq),
        grid=(b, s // tq),
        in_specs=[pl.BlockSpec(memory_space=pltpu.SMEM), qspec, kvspec, kvspec],
        out_specs=qspec,
        out_shape=jax.ShapeDtypeStruct((b, s, SB_WIDTH), BF16),
        scratch_shapes=[pltpu.VMEM((n_pairs, 2 * tq, LANES), BF16),
                        pltpu.VMEM((SB_HEADS, tq, tq), F32),
                        pltpu.VMEM((SB_HEADS * tq, tq), BF16),
                        pltpu.VMEM((n_pairs, 2 * tq, tq), BF16),
                        pltpu.VMEM((SB_HEADS, tq, LANES), F32),
                        pltpu.VMEM((n_pairs, 2 * tq, LANES), F32)],
        compiler_params=_compiler_params(("parallel", "arbitrary")),
        name="sb_prompt",
    )(bias, q, kb, vb)


def _sb_sample_kernel(pt_ref, bias_ref, q_ref, kn_ref, vn_ref, *rest, pages_per_step, t_new):
    del pt_ref
    kpages = rest[:pages_per_step]
    vpages = rest[pages_per_step:2 * pages_per_step]
    o_ref = rest[2 * pages_per_step]
    qbd_ref, biasrow_ref, carry_ref, acc_ref = rest[2 * pages_per_step + 1:]
    g = pl.program_id(1)
    rows = SB_HEADS * t_new
    tmat = _suffix_sum_matrix(PAGE_SIZE)

    def weights(zs, causal):
        ws, c = _sb_weights([z + biasrow_ref[...] for z in zs], carry_ref[...], tmat, causal)
        carry_ref[...] = c
        return ws

    @pl.when(g == 0)
    def _():
        r = lax.broadcasted_iota(jnp.int32, (rows, SB_WIDTH), 0)
        c = lax.broadcasted_iota(jnp.int32, (rows, SB_WIDTH), 1)
        qrep = jnp.concatenate([q_ref[0].astype(F32)] * SB_HEADS, axis=0)
        qbd_ref[...] = jnp.where(c // SB_HEAD_DIM == r // t_new, qrep, 0.0).astype(BF16)
        rr = lax.broadcasted_iota(jnp.int32, (rows, PAGE_SIZE), 0)
        brow = jnp.zeros((rows, PAGE_SIZE), F32)
        for h in range(SB_HEADS):
            brow = jnp.where(rr // t_new == h, bias_ref[h], brow)
        biasrow_ref[...] = brow
        carry_ref[...] = jnp.zeros_like(carry_ref)
        acc_ref[...] = jnp.zeros_like(acc_ref)
        pad = jnp.zeros((PAGE_SIZE - t_new, SB_WIDTH), F32)
        kn = jnp.concatenate([kn_ref[0], pad], axis=0).astype(BF16)
        vn = jnp.concatenate([vn_ref[0], pad], axis=0).astype(BF16)
        cc = lax.broadcasted_iota(jnp.int32, (rows, PAGE_SIZE), 1)
        w_new, = weights([lax.dot_general(qbd_ref[...], kn, _NT, preferred_element_type=F32)],
                         cc < rr % t_new)
        acc_ref[...] += jnp.dot(w_new, vn, preferred_element_type=F32)

    qbd = qbd_ref[...]
    ws = weights([jnp.dot(qbd, kp[0].reshape(SB_WIDTH, PAGE_SIZE).astype(BF16), preferred_element_type=F32)
                  for kp in kpages], None)
    vt = jnp.concatenate([vp[0].reshape(SB_WIDTH, PAGE_SIZE).astype(BF16) for vp in vpages], axis=1)
    acc_ref[...] += lax.dot_general(jnp.concatenate(ws, axis=1), vt, _NT, preferred_element_type=F32)

    @pl.when(g == pl.num_programs(1) - 1)
    def _():
        c = lax.broadcasted_iota(jnp.int32, (t_new, SB_WIDTH), 1)
        out = jnp.zeros((t_new, SB_WIDTH), F32)
        for h in range(SB_HEADS):
            out = jnp.where(c // SB_HEAD_DIM == h, acc_ref[h * t_new:(h + 1) * t_new, :], out)
        o_ref[0] = out.astype(o_ref.dtype)


def _sb_sample(page_table, bias, q, k_new, v_new, cache_k, cache_v, pages_per_step):
    b, t_new, _ = q.shape
    n_pages = page_table.shape[1]
    assert n_pages % pages_per_step == 0 and t_new % SUBLANES == 0
    steps = n_pages // pages_per_step
    rows = SB_HEADS * t_new

    def page_spec(p):
        return pl.BlockSpec((1, SB_HEADS, SB_HEAD_DIM, PAGE_SIZE),
                            lambda bi, g, pt: (pt[bi, n_pages - 1 - (g * pages_per_step + p)], 0, 0, 0))

    tok = lambda dt: pl.BlockSpec((1, t_new, SB_WIDTH), lambda bi, g, pt: (bi, 0, 0))
    grid_spec = pltpu.PrefetchScalarGridSpec(
        num_scalar_prefetch=1,
        grid=(b, steps),
        in_specs=[pl.BlockSpec(memory_space=pltpu.SMEM), tok(BF16), tok(F32), tok(F32)]
        + [page_spec(p) for p in range(pages_per_step)] * 2,
        out_specs=pl.BlockSpec((1, t_new, SB_WIDTH), lambda bi, g, pt: (bi, 0, 0)),
        scratch_shapes=[pltpu.VMEM((rows, SB_WIDTH), BF16),
                        pltpu.VMEM((rows, PAGE_SIZE), F32),
                        pltpu.VMEM((rows, PAGE_SIZE), F32),
                        pltpu.VMEM((rows, SB_WIDTH), F32)],
    )
    return pl.pallas_call(
        functools.partial(_sb_sample_kernel, pages_per_step=pages_per_step, t_new=t_new),
        grid_spec=grid_spec,
        out_shape=jax.ShapeDtypeStruct((b, t_new, SB_WIDTH), F32),
        compiler_params=_compiler_params(("parallel", "arbitrary")),
        name="sb_sample",
    )(page_table, bias, q, k_new, v_new,
      *([cache_k] * pages_per_step), *([cache_v] * pages_per_step))


def _layer_norm(x, g, b):
    mu = jnp.mean(x, axis=-1, keepdims=True)
    xc = x - mu
    var = jnp.mean(xc * xc, axis=-1, keepdims=True)
    return xc * lax.rsqrt(var + LN_EPS) * g + b


def _mix_kernel(x_ref, attn_ref, u_ref, prev_ref, wpool_ref, pscale_ref, wout_ref, g_ref, b_ref, o_ref,
                *, alpha, pos0, zero_first_prev):
    bb, t, _ = u_ref.shape
    i = pl.program_id(1)
    prev = prev_ref[...]
    if zero_first_prev:
        prev = jnp.where(i == 0, 0.0, prev)
    ext = jnp.concatenate([prev, u_ref[...]], axis=1)
    pos = pos0 + i * t + lax.broadcasted_iota(jnp.int32, (1, t, 1), 1)
    mixed = []
    for gi, win in enumerate(POOL_WINDOWS):
        sl = slice(gi * POOL_GROUP_DIM, (gi + 1) * POOL_GROUP_DIM)
        acc = ext[:, :, sl]
        span = 1
        while span < win:
            acc = acc[:, span:, :] + acc[:, :acc.shape[1] - span, :]
            span *= 2
        tok = ext[:, POOL_PREV_ROWS:, sl]
        inv = 1.0 / jnp.minimum(win, pos + 1).astype(F32)
        pooled = acc[:, acc.shape[1] - t:, :] * inv - tok
        pooled = pooled.reshape(bb * t, POOL_GROUP_DIM).astype(BF16)
        mixed.append(jnp.dot(pooled, wpool_ref[gi], preferred_element_type=F32))
    pool_out = (jnp.concatenate(mixed, axis=-1) * pscale_ref[...]).astype(BF16)
    attn = attn_ref[...].reshape(bb * t, SB_WIDTH).astype(BF16)
    mix = (jnp.dot(attn, wout_ref[0:SB_WIDTH, :], preferred_element_type=F32)
           + jnp.dot(pool_out, wout_ref[SB_WIDTH:, :], preferred_element_type=F32))
    x = x_ref[...].reshape(bb * t, x_ref.shape[2])
    o_ref[...] = _layer_norm(alpha * x + mix, g_ref[...], b_ref[...]).reshape(o_ref.shape)


def _mix(x, attn, u, prev, wpool_bf, pscale, wout_bf, ln_g, ln_b, *, bb, t, alpha, pos0, zero_first_prev):
    b, s, d = x.shape
    assert b % bb == 0 and s % t == 0 and t % SUBLANES == 0
    if zero_first_prev:
        assert t % POOL_PREV_ROWS == 0
        prev_map = lambda bi, i: (bi, jnp.maximum(i * (t // POOL_PREV_ROWS) - 1, 0), 0)
    else:
        assert s == t
        prev_map = lambda bi, i: (bi, 0, 0)
    tile = lambda w: pl.BlockSpec((bb, t, w), lambda bi, i: (bi, i, 0))
    const2 = lambda shape: pl.BlockSpec(shape, lambda bi, i: (0, 0))
    return pl.pallas_call(
        functools.partial(_mix_kernel, alpha=alpha, pos0=pos0, zero_first_prev=zero_first_prev),
        grid=(b // bb, s // t),
        in_specs=[tile(d), tile(SB_WIDTH), tile(POOL_WIDTH),
                  pl.BlockSpec((bb, POOL_PREV_ROWS, POOL_WIDTH), prev_map),
                  pl.BlockSpec(wpool_bf.shape, lambda bi, i: (0, 0, 0)),
                  const2((1, POOL_WIDTH)), const2(wout_bf.shape), const2((1, d)), const2((1, d))],
        out_specs=tile(d),
        out_shape=jax.ShapeDtypeStruct((b, s, d), F32),
        compiler_params=_compiler_params(("parallel", "arbitrary")),
        name="mix_ln1",
    )(x, attn, u, prev, wpool_bf, pscale, wout_bf, ln_g, ln_b)


def _top_values(s, count):
    vals = []
    for r in range(count):
        m = jnp.max(s, axis=0, keepdims=True)
        vals.append(m)
        if r + 1 < count:
            s = jnp.where(s == m, NEG_BIG, s)
    return vals


def _stack_rows(vals, rows):
    tm = vals[0].shape[1]
    ridx = lax.broadcasted_iota(jnp.int32, (rows, tm), 0)
    tile = jnp.full((rows, tm), NEG_BIG, F32)
    for r, v in enumerate(vals):
        tile = jnp.where(ridx == r, v, tile)
    return tile


def _peer_route(xb, wqt_ref, keys_ref, s2_ref, cut_ref, ea_ref, eb_ref):
    n_top = PEER_TOPK + 1
    for h in range(PEER_HEADS):
        scores = []
        for p in range(2):
            hp = 2 * h + p
            qt = lax.dot_general(wqt_ref[hp * PEER_HALF:(hp + 1) * PEER_HALF, :], xb,
                                 (((1,), (1,)), ((), ())), preferred_element_type=F32)
            scores.append(jnp.dot(keys_ref[hp], qt.astype(BF16), preferred_element_type=F32))
        s1, s2 = scores
        a = _top_values(s1, n_top)
        b = _top_values(s2, n_top)
        b_full = _stack_rows(b, 3 * SUBLANES)
        b_head = b_full[0:SUBLANES, :]
        cand = jnp.concatenate([a[0] + b_full] + [a[k] + b_head for k in range(1, n_top)], axis=0)
        cv = _top_values(cand, n_top)
        cut = 0.5 * (cv[PEER_TOPK - 1] + cv[PEER_TOPK])
        z = jnp.ones_like(cv[0])
        for r in range(1, PEER_TOPK):
            z = z + jnp.exp(cv[r] - cv[0])
        s2_ref[h] = s2
        cut_ref[h] = cut - s1
        ea_ref[h] = jnp.exp(s1 - a[0]) / z
        eb_ref[h] = jnp.exp(s2 - b[0])


def _peer_chunk(act_ref, i0, vt_ref, col0, s2_ref, cut_ref, ea_ref, eb_ref, rows_per_chunk):
    tm = act_ref.shape[1]
    total = None
    for pair in range(rows_per_chunk // 2):
        coef = []
        for ii in (2 * pair, 2 * pair + 1):
            gate = jnp.zeros((PEER_N_KEYS, tm), F32)
            for h in range(PEER_HEADS):
                cut_i = cut_ref[h, pl.ds(i0 + ii, 1), :]
                ea_i = ea_ref[h, pl.ds(i0 + ii, 1), :]
                gate = gate + ea_i * jnp.where(s2_ref[h] >= cut_i, eb_ref[h], 0.0)
            a = act_ref[ii * PEER_N_KEYS:(ii + 1) * PEER_N_KEYS, :]
            gelu = 0.5 * a * (1.0 + lax.erf(a * (2.0 ** -0.5)))
            coef.append((gate * gelu).astype(BF16))
        lo = col0 + 2 * pair * PEER_N_KEYS
        part = jnp.dot(vt_ref[:, lo:lo + 2 * PEER_N_KEYS], jnp.concatenate(coef, axis=0),
                       preferred_element_type=F32)
        total = part if total is None else total + part
    return total


def _peer_kernel(x_ref, wqt_ref, keys_ref, u0_ref, ua_ref, ub_ref, vt_ref, g_ref, b_ref, o_ref,
                 xb_ref, s2_ref, cut_ref, ea_ref, eb_ref, act_ref, acc_ref, *, alpha, rows_per_chunk):
    c = pl.program_id(1)
    chunk = rows_per_chunk * PEER_N_KEYS
    route = (s2_ref, cut_ref, ea_ref, eb_ref)

    @pl.when(c == 0)
    def _():
        xb_ref[...] = x_ref[...].astype(BF16)
        act_ref[0] = lax.dot_general(u0_ref[...], xb_ref[...], _NT, preferred_element_type=F32)
        _peer_route(xb_ref[...], wqt_ref, keys_ref, *route)
        acc_ref[...] = jnp.zeros_like(acc_ref)

    i0 = pl.multiple_of(c * 2 * rows_per_chunk, 2 * rows_per_chunk)
    act_ref[1] = lax.dot_general(ua_ref[...], xb_ref[...], _NT, preferred_element_type=F32)
    acc_ref[...] += _peer_chunk(act_ref.at[0], i0, vt_ref, 0, *route, rows_per_chunk)
    act_ref[0] = lax.dot_general(ub_ref[...], xb_ref[...], _NT, preferred_element_type=F32)
    acc_ref[...] += _peer_chunk(act_ref.at[1], i0 + rows_per_chunk, vt_ref, chunk, *route, rows_per_chunk)

    @pl.when(c == pl.num_programs(1) - 1)
    def _():
        peer = acc_ref[...].T
        o_ref[...] = _layer_norm(alpha * x_ref[...] + peer, g_ref[...], b_ref[...])


def _peer(x1, wqt_bf, keys_bf, u_bf, vt_bf, ln_g, ln_b, *, tm, rows_per_chunk, alpha):
    n, d = x1.shape
    n_exp = u_bf.shape[0]
    chunk = rows_per_chunk * PEER_N_KEYS
    assert n % tm == 0 and n_exp % (2 * chunk) == 0 and rows_per_chunk % SUBLANES == 0
    n_chunks = n_exp // chunk
    const = lambda shape: pl.BlockSpec(shape, lambda i, c: (0,) * len(shape))
    route = pltpu.VMEM((PEER_HEADS, PEER_N_KEYS, tm), F32)
    return pl.pallas_call(
        functools.partial(_peer_kernel, alpha=alpha, rows_per_chunk=rows_per_chunk),
        grid=(n // tm, n_chunks // 2),
        in_specs=[pl.BlockSpec((tm, d), lambda i, c: (i, 0)),
                  const(wqt_bf.shape), const(keys_bf.shape),
                  pl.BlockSpec((chunk, d), lambda i, c: (0, 0)),
                  pl.BlockSpec((chunk, d), lambda i, c: (2 * c + 1, 0)),
                  pl.BlockSpec((chunk, d), lambda i, c: (jnp.minimum(2 * c + 2, n_chunks - 1), 0)),
                  pl.BlockSpec((d, 2 * chunk), lambda i, c: (0, c)),
                  const((1, d)), const((1, d))],
        out_specs=pl.BlockSpec((tm, d), lambda i, c: (i, 0)),
        out_shape=jax.ShapeDtypeStruct((n, d), F32),
        scratch_shapes=[pltpu.VMEM((tm, d), BF16), route, route, route, route,
                        pltpu.VMEM((2, chunk, tm), F32), pltpu.VMEM((d, tm), F32)],
        compiler_params=_compiler_params(("parallel", "arbitrary")),
        name="peer_ln2",
    )(x1, wqt_bf, keys_bf, u_bf, u_bf, u_bf, vt_bf, ln_g, ln_b)


def _pick(n, prefs):
    for p in prefs:
        if n % p == 0:
            return p
    return n


def kernel(x_prompt, x_sample, cache_k, cache_v, state_pool, page_table, w_in, sb_bias, w_out, w_pool,
           pool_scale, ln1_g, ln1_b, w_query, sub_keys, u_table, v_table, ln2_g, ln2_b):
    depth = w_in.shape[0]
    b, s, d = x_prompt.shape
    db, t_new, _ = x_sample.shape
    n_pages = page_table.shape[1]
    past_len = n_pages * PAGE_SIZE
    alpha = (2.0 * depth) ** 0.25
    n_pool = cache_k.shape[1]

    hp, hs = x_prompt, x_sample
    outs = [[] for _ in range(6)]
    for l in range(depth):
        w_in_bf = w_in[l].astype(BF16)
        wout_bf = w_out[l].astype(BF16)
        wpool_bf = w_pool[l].astype(BF16)
        wqt_bf = w_query[l].T.astype(BF16)
        keys_bf = sub_keys[l].reshape(2 * PEER_HEADS, PEER_N_KEYS, PEER_HALF).astype(BF16)
        u_bf = u_table[l].astype(BF16)
        vt_bf = v_table[l].T.astype(BF16)
        pscale = pool_scale[l].reshape(1, POOL_WIDTH)
        g1, b1 = ln1_g[l].reshape(1, d), ln1_b[l].reshape(1, d)
        g2, b2 = ln2_g[l].reshape(1, d), ln2_b[l].reshape(1, d)
        bias = sb_bias[l].astype(F32)
        peer = functools.partial(_peer, wqt_bf=wqt_bf, keys_bf=keys_bf, u_bf=u_bf, vt_bf=vt_bf,
                                 ln_g=g2, ln_b=b2, rows_per_chunk=SUBLANES, alpha=alpha)

        n = b * s
        q, k, v, u, kb, vb = _in_proj(hp.reshape(n, d), w_in_bf, _pick(n, (512, 256, 128)))
        attn = _sb_prompt(bias, q.reshape(b, s, SB_WIDTH), kb.reshape(b, s, SB_WIDTH),
                          vb.reshape(b, s, SB_WIDTH), _pick(s, (256, 128)))
        u3 = u.reshape(b, s, POOL_WIDTH)
        x1 = _mix(hp, attn, u3, u3, wpool_bf, pscale, wout_bf, g1, b1,
                  bb=1, t=_pick(s, (512, 256, 128)), alpha=alpha, pos0=0, zero_first_prev=True)
        hp = peer(x1.reshape(n, d), tm=_pick(n, (256, 128))).reshape(b, s, d)
        outs[0].append(k.reshape(b, s, SB_HEADS, SB_HEAD_DIM))
        outs[1].append(v.reshape(b, s, SB_HEADS, SB_HEAD_DIM))
        outs[2].append(u3[:, s - POOL_STATE_LEN:])

        ns = db * t_new
        qs, ks, vs, us, _, _ = _in_proj(hs.reshape(ns, d), w_in_bf, _pick(ns, (512, 256, 128)))
        attn_s = _sb_sample(page_table, bias, qs.reshape(db, t_new, SB_WIDTH),
                            ks.reshape(db, t_new, SB_WIDTH), vs.reshape(db, t_new, SB_WIDTH),
                            cache_k[l].transpose(0, 2, 3, 1), cache_v[l].transpose(0, 2, 3, 1),
                            _pick(n_pages, (8, 4, 2, 1)))
        us3 = us.reshape(db, t_new, POOL_WIDTH)
        state = state_pool[l].astype(F32)
        prev = jnp.pad(state, ((0, 0), (POOL_PREV_ROWS - POOL_STATE_LEN, 0), (0, 0)))
        x1s = _mix(hs, attn_s, us3, prev, wpool_bf, pscale, wout_bf, g1, b1,
                   bb=_pick(db, (16, 8, 4, 2, 1)), t=t_new, alpha=alpha, pos0=past_len, zero_first_prev=False)
        hs = peer(x1s.reshape(ns, d), tm=_pick(ns, (256, 128))).reshape(db, t_new, d)
        outs[3].append(ks.reshape(db, t_new, SB_HEADS, SB_HEAD_DIM))
        outs[4].append(vs.reshape(db, t_new, SB_HEADS, SB_HEAD_DIM))
        outs[5].append(jnp.concatenate([state, us3], axis=1)[:, -POOL_STATE_LEN:])

    return (hp, hs, jnp.stack(outs[0]), jnp.stack(outs[1]), jnp.stack(outs[2]),
            jnp.stack(outs[3]), jnp.stack(outs[4]), jnp.stack(outs[5]))
```

```python
import functools
import math

import jax
import jax.numpy as jnp
from jax import lax
from jax.experimental import pallas as pl
from jax.experimental.pallas import tpu as pltpu

F32 = jnp.float32
BF16 = jnp.bfloat16

SB_HEADS = 8
SB_HEAD_DIM = 64
SB_WIDTH = SB_HEADS * SB_HEAD_DIM
POOL_WINDOWS = (2, 4, 8, 16)
POOL_GROUP_DIM = 128
POOL_WIDTH = len(POOL_WINDOWS) * POOL_GROUP_DIM
POOL_STATE_LEN = max(POOL_WINDOWS) - 1
POOL_PREV_ROWS = 16
PAGE_SIZE = 128
PEER_HEADS = 8
PEER_N_KEYS = 128
PEER_TOPK = 16
PEER_HALF = 128
LN_EPS = 1e-5

LANES = 128
SUBLANES = 8
VMEM_LIMIT_BYTES = 56 * 1024 * 1024

NEG_BIG = -1e30
_NT = (((1,), (1,)), ((), ()))


def _compiler_params(semantics):
    return pltpu.CompilerParams(dimension_semantics=semantics, vmem_limit_bytes=VMEM_LIMIT_BYTES)


def _proj_kernel(x_ref, w_ref, wkvt_ref, q_ref, k_ref, v_ref, u_ref, kb_ref, vb_ref, *, tokens_last):
    xb = x_ref[...].astype(BF16)
    w = SB_WIDTH
    q = jnp.dot(xb, w_ref[:, 0:w], preferred_element_type=F32)
    q_ref[...] = (q * (SB_HEAD_DIM ** -0.5)).astype(BF16)
    k = jnp.dot(xb, w_ref[:, w:2 * w], preferred_element_type=F32)
    kb_ref[...] = k.astype(BF16)
    v = jnp.dot(xb, w_ref[:, 2 * w:3 * w], preferred_element_type=F32)
    vb_ref[...] = v.astype(BF16)
    u_ref[...] = jnp.dot(xb, w_ref[:, 3 * w:], preferred_element_type=F32)
    if tokens_last:
        k_ref[0] = lax.dot_general(wkvt_ref[0:w, :], xb, _NT, preferred_element_type=F32)
        v_ref[0] = lax.dot_general(wkvt_ref[w:2 * w, :], xb, _NT, preferred_element_type=F32)
    else:
        k_ref[...] = k
        v_ref[...] = v


def _in_proj(x2d, w_in_bf, wkvt_bf, tm, seq=None):
    n, d = x2d.shape
    pw = w_in_bf.shape[1]
    row = lambda i: (i, 0)
    out_block = pl.BlockSpec((tm, SB_WIDTH), row)
    if seq is None:
        kv_block, kv_shape = out_block, (n, SB_WIDTH)
    else:
        assert seq % tm == 0 and n % seq == 0
        per_seq = seq // tm
        kv_block = pl.BlockSpec((1, SB_WIDTH, tm), lambda i: (i // per_seq, 0, i % per_seq))
        kv_shape = (n // seq, SB_WIDTH, seq)
    return pl.pallas_call(
        functools.partial(_proj_kernel, tokens_last=seq is not None),
        grid=(n // tm,),
        in_specs=[pl.BlockSpec((tm, d), row), pl.BlockSpec((d, pw), lambda i: (0, 0)),
                  pl.BlockSpec(wkvt_bf.shape, lambda i: (0, 0))],
        out_specs=[out_block, kv_block, kv_block, out_block, out_block, out_block],
        out_shape=[jax.ShapeDtypeStruct((n, SB_WIDTH), BF16),
                   jax.ShapeDtypeStruct(kv_shape, F32),
                   jax.ShapeDtypeStruct(kv_shape, F32),
                   jax.ShapeDtypeStruct((n, POOL_WIDTH), F32),
                   jax.ShapeDtypeStruct((n, SB_WIDTH), BF16),
                   jax.ShapeDtypeStruct((n, SB_WIDTH), BF16)],
        compiler_params=_compiler_params(("parallel",)),
        name="in_proj",
    )(x2d, w_in_bf, wkvt_bf)


def _suffix_sum_matrix(tk):
    r = lax.broadcasted_iota(jnp.int32, (tk, 2 * tk), 0)
    c = lax.broadcasted_iota(jnp.int32, (tk, 2 * tk), 1)
    return jnp.where((r > c) | (c >= tk), 1.0, 0.0).astype(BF16)


def _sb_weights(zs, carry, tmat, causal):
    rows, tk = zs[0].shape
    log1m = [jnp.minimum(-z, 0.0) - jnp.log(1.0 + jnp.exp(-jnp.abs(z))) for z in zs]
    if causal is not None:
        log1m = [jnp.where(causal, l, 0.0) for l in log1m]
    sums = jnp.dot(jnp.concatenate(log1m, axis=0).astype(BF16), tmat, preferred_element_type=F32)
    ws = []
    for p, (z, l) in enumerate(zip(zs, log1m)):
        sp = sums[p * rows:(p + 1) * rows]
        w = jnp.exp((z + l) + (sp[:, :tk] + carry))
        if causal is not None:
            w = jnp.where(causal, w, 0.0)
        ws.append(w.astype(BF16))
        carry = carry + sp[:, tk:]
    return ws, carry


def _sb_prompt_kernel(bias_ref, q_ref, k_ref, v_ref, o_ref,
                      qs_ref, zl_ref, l_ref, w_ref, carry_ref, acc_ref, *, tq):
    qi = pl.program_id(1)
    n_pairs = SB_HEADS // 2
    r = lax.broadcasted_iota(jnp.int32, (tq, tq), 0)
    c = lax.broadcasted_iota(jnp.int32, (tq, tq), 1)
    strict = c < r
    tmat = jnp.where(r > c, 1.0, 0.0).astype(BF16)
    lane = lax.broadcasted_iota(jnp.int32, (tq, LANES), 1)
    low = lane < SB_HEAD_DIM
    for hp in range(n_pairs):
        q = q_ref[0, :, hp * LANES:(hp + 1) * LANES]
        zero = jnp.zeros_like(q)
        qs_ref[hp, 0:tq, :] = jnp.where(low, q, zero)
        qs_ref[hp, tq:2 * tq, :] = jnp.where(low, zero, q)
    carry_ref[...] = jnp.zeros_like(carry_ref)
    acc_ref[...] = jnp.zeros_like(acc_ref)

    def block(j, causal):
        start = pl.multiple_of(j * tq, tq)
        for hp in range(n_pairs):
            kj = k_ref[0, pl.ds(start, tq), hp * LANES:(hp + 1) * LANES]
            z2 = lax.dot_general(qs_ref[hp], kj, (((1,), (1,)), ((), ())), preferred_element_type=F32)
            for h2 in range(2):
                h = 2 * hp + h2
                z = z2[h2 * tq:(h2 + 1) * tq, :] + bias_ref[h]
                log1m = jnp.minimum(-z, 0.0) - jnp.log(1.0 + jnp.exp(-jnp.abs(z)))
                if causal:
                    log1m = jnp.where(strict, log1m, 0.0)
                zl_ref[h] = z + log1m
                l_ref[h * tq:(h + 1) * tq, :] = log1m.astype(BF16)
        later = jnp.dot(l_ref[...], tmat, preferred_element_type=F32)
        for hp in range(n_pairs):
            for h2 in range(2):
                h = 2 * hp + h2
                rest = later[h * tq:(h + 1) * tq, :]
                first = l_ref[h * tq:(h + 1) * tq, 0:1].astype(F32)
                rowsum = jnp.broadcast_to(rest[:, 0:1] + first, (tq, LANES))
                carry = carry_ref[h]
                w = jnp.exp(zl_ref[h] + rest + jnp.concatenate([carry] * (tq // LANES), axis=1))
                if causal:
                    w = jnp.where(strict, w, 0.0)
                w_ref[hp, h2 * tq:(h2 + 1) * tq, :] = w.astype(BF16)
                carry_ref[h] = carry + rowsum
        for hp in range(n_pairs):
            vj = v_ref[0, pl.ds(start, tq), hp * LANES:(hp + 1) * LANES]
            acc_ref[hp] += jnp.dot(w_ref[hp], vj, preferred_element_type=F32)

    block(qi, True)

    def body(it, _):
        block(qi - 1 - it, False)
        return 0

    lax.fori_loop(0, qi, body, 0)
    for hp in range(n_pairs):
        o_ref[0, :, hp * LANES:(hp + 1) * LANES] = jnp.where(
            low, acc_ref[hp, 0:tq, :], acc_ref[hp, tq:2 * tq, :]).astype(o_ref.dtype)


def _sb_prompt(bias, q, kb, vb, tq):
    b, s, _ = q.shape
    assert tq % LANES == 0 and s % tq == 0
    qspec = pl.BlockSpec((1, tq, SB_WIDTH), lambda bi, qi: (bi, qi, 0))
    kvspec = pl.BlockSpec((1, s, SB_WIDTH), lambda bi, qi: (bi, 0, 0))
    n_pairs = SB_HEADS // 2
    return pl.pallas_call(
        functools.partial(_sb_prompt_kernel, tq=tq),
        grid=(b, s // tq),
        in_specs=[pl.BlockSpec(memory_space=pltpu.SMEM), qspec, kvspec, kvspec],
        out_specs=qspec,
        out_shape=jax.ShapeDtypeStruct((b, s, SB_WIDTH), BF16),
        scratch_shapes=[pltpu.VMEM((n_pairs, 2 * tq, LANES), BF16),
                        pltpu.VMEM((SB_HEADS, tq, tq), F32),
                        pltpu.VMEM((SB_HEADS * tq, tq), BF16),
                        pltpu.VMEM((n_pairs, 2 * tq, tq), BF16),
                        pltpu.VMEM((SB_HEADS, tq, LANES), F32),
                        pltpu.VMEM((n_pairs, 2 * tq, LANES), F32)],
        compiler_params=_compiler_params(("parallel", "arbitrary")),
        name="sb_prompt",
    )(bias, q, kb, vb)


def _sb_sample_kernel(pt_ref, bias_ref, q_ref, kn_ref, vn_ref, *rest, pages_per_step, t_new):
    del pt_ref
    kpages = rest[:pages_per_step]
    vpages = rest[pages_per_step:2 * pages_per_step]
    o_ref = rest[2 * pages_per_step]
    qbd_ref, biasrow_ref, carry_ref, acc_ref = rest[2 * pages_per_step + 1:]
    g = pl.program_id(1)
    rows = SB_HEADS * t_new
    tmat = _suffix_sum_matrix(PAGE_SIZE)

    def weights(zs, causal):
        ws, c = _sb_weights([z + biasrow_ref[...] for z in zs], carry_ref[...], tmat, causal)
        carry_ref[...] = c
        return ws

    @pl.when(g == 0)
    def _():
        r = lax.broadcasted_iota(jnp.int32, (rows, SB_WIDTH), 0)
        c = lax.broadcasted_iota(jnp.int32, (rows, SB_WIDTH), 1)
        qrep = jnp.concatenate([q_ref[0].astype(F32)] * SB_HEADS, axis=0)
        qbd_ref[...] = jnp.where(c // SB_HEAD_DIM == r // t_new, qrep, 0.0).astype(BF16)
        rr = lax.broadcasted_iota(jnp.int32, (rows, PAGE_SIZE), 0)
        brow = jnp.zeros((rows, PAGE_SIZE), F32)
        for h in range(SB_HEADS):
            brow = jnp.where(rr // t_new == h, bias_ref[h], brow)
        biasrow_ref[...] = brow
        carry_ref[...] = jnp.zeros_like(carry_ref)
        acc_ref[...] = jnp.zeros_like(acc_ref)
        pad = jnp.zeros((PAGE_SIZE - t_new, SB_WIDTH), F32)
        kn = jnp.concatenate([kn_ref[0], pad], axis=0).astype(BF16)
        vn = jnp.concatenate([vn_ref[0], pad], axis=0).astype(BF16)
        cc = lax.broadcasted_iota(jnp.int32, (rows, PAGE_SIZE), 1)
        w_new, = weights([lax.dot_general(qbd_ref[...], kn, _NT, preferred_element_type=F32)],
                         cc < rr % t_new)
        acc_ref[...] += jnp.dot(w_new, vn, preferred_element_type=F32)

    qbd = qbd_ref[...]
    ws = weights([jnp.dot(qbd, kp[0].reshape(SB_WIDTH, PAGE_SIZE).astype(BF16), preferred_element_type=F32)
                  for kp in kpages], None)
    vt = jnp.concatenate([vp[0].reshape(SB_WIDTH, PAGE_SIZE).astype(BF16) for vp in vpages], axis=1)
    acc_ref[...] += lax.dot_general(jnp.concatenate(ws, axis=1), vt, _NT, preferred_element_type=F32)

    @pl.when(g == pl.num_programs(1) - 1)
    def _():
        c = lax.broadcasted_iota(jnp.int32, (t_new, SB_WIDTH), 1)
        out = jnp.zeros((t_new, SB_WIDTH), F32)
        for h in range(SB_HEADS):
            out = jnp.where(c // SB_HEAD_DIM == h, acc_ref[h * t_new:(h + 1) * t_new, :], out)
        o_ref[0] = out.astype(o_ref.dtype)


def _sb_sample(page_table, bias, q, k_new, v_new, cache_k, cache_v, pages_per_step):
    b, t_new, _ = q.shape
    n_pages = page_table.shape[1]
    assert n_pages % pages_per_step == 0 and t_new % SUBLANES == 0
    steps = n_pages // pages_per_step
    rows = SB_HEADS * t_new

    def page_spec(p):
        return pl.BlockSpec((1, SB_HEADS, SB_HEAD_DIM, PAGE_SIZE),
                            lambda bi, g, pt: (pt[bi, n_pages - 1 - (g * pages_per_step + p)], 0, 0, 0))

    tok = lambda dt: pl.BlockSpec((1, t_new, SB_WIDTH), lambda bi, g, pt: (bi, 0, 0))
    grid_spec = pltpu.PrefetchScalarGridSpec(
        num_scalar_prefetch=1,
        grid=(b, steps),
        in_specs=[pl.BlockSpec(memory_space=pltpu.SMEM), tok(BF16), tok(F32), tok(F32)]
        + [page_spec(p) for p in range(pages_per_step)] * 2,
        out_specs=pl.BlockSpec((1, t_new, SB_WIDTH), lambda bi, g, pt: (bi, 0, 0)),
        scratch_shapes=[pltpu.VMEM((rows, SB_WIDTH), BF16),
                        pltpu.VMEM((rows, PAGE_SIZE), F32),
                        pltpu.VMEM((rows, PAGE_SIZE), F32),
                        pltpu.VMEM((rows, SB_WIDTH), F32)],
    )
    return pl.pallas_call(
        functools.partial(_sb_sample_kernel, pages_per_step=pages_per_step, t_new=t_new),
        grid_spec=grid_spec,
        out_shape=jax.ShapeDtypeStruct((b, t_new, SB_WIDTH), F32),
        compiler_params=_compiler_params(("parallel", "arbitrary")),
        name="sb_sample",
    )(page_table, bias, q, k_new, v_new,
      *([cache_k] * pages_per_step), *([cache_v] * pages_per_step))


def _layer_norm(x, g, b):
    mu = jnp.mean(x, axis=-1, keepdims=True)
    xc = x - mu
    var = jnp.mean(xc * xc, axis=-1, keepdims=True)
    return xc * lax.rsqrt(var + LN_EPS) * g + b


def _mix_kernel(x_ref, attn_ref, u_ref, prev_ref, wpool_ref, pscale_ref, wout_ref, g_ref, b_ref, o_ref,
                *, alpha, pos0, zero_first_prev):
    bb, t, _ = u_ref.shape
    i = pl.program_id(1)
    prev = prev_ref[...]
    if zero_first_prev:
        prev = jnp.where(i == 0, 0.0, prev)
    ext = jnp.concatenate([prev, u_ref[...]], axis=1)
    pos = pos0 + i * t + lax.broadcasted_iota(jnp.int32, (1, t, 1), 1)
    mixed = []
    for gi, win in enumerate(POOL_WINDOWS):
        sl = slice(gi * POOL_GROUP_DIM, (gi + 1) * POOL_GROUP_DIM)
        acc = ext[:, :, sl]
        span = 1
        while span < win:
            acc = acc[:, span:, :] + acc[:, :acc.shape[1] - span, :]
            span *= 2
        tok = ext[:, POOL_PREV_ROWS:, sl]
        inv = 1.0 / jnp.minimum(win, pos + 1).astype(F32)
        pooled = acc[:, acc.shape[1] - t:, :] * inv - tok
        pooled = pooled.reshape(bb * t, POOL_GROUP_DIM).astype(BF16)
        mixed.append(jnp.dot(pooled, wpool_ref[gi], preferred_element_type=F32))
    pool_out = (jnp.concatenate(mixed, axis=-1) * pscale_ref[...]).astype(BF16)
    attn = attn_ref[...].reshape(bb * t, SB_WIDTH).astype(BF16)
    mix = (jnp.dot(attn, wout_ref[0:SB_WIDTH, :], preferred_element_type=F32)
           + jnp.dot(pool_out, wout_ref[SB_WIDTH:, :], preferred_element_type=F32))
    x = x_ref[...].reshape(bb * t, x_ref.shape[2])
    o_ref[...] = _layer_norm(alpha * x + mix, g_ref[...], b_ref[...]).reshape(o_ref.shape)


def _mix(x, attn, u, prev, wpool_bf, pscale, wout_bf, ln_g, ln_b, *, bb, t, alpha, pos0, zero_first_prev):
    b, s, d = x.shape
    assert b % bb == 0 and s % t == 0 and t % SUBLANES == 0
    if zero_first_prev:
        assert t % POOL_PREV_ROWS == 0
        prev_map = lambda bi, i: (bi, jnp.maximum(i * (t // POOL_PREV_ROWS) - 1, 0), 0)
    else:
        assert s == t
        prev_map = lambda bi, i: (bi, 0, 0)
    tile = lambda w: pl.BlockSpec((bb, t, w), lambda bi, i: (bi, i, 0))
    const2 = lambda shape: pl.BlockSpec(shape, lambda bi, i: (0, 0))
    return pl.pallas_call(
        functools.partial(_mix_kernel, alpha=alpha, pos0=pos0, zero_first_prev=zero_first_prev),
        grid=(b // bb, s // t),
        in_specs=[tile(d), tile(SB_WIDTH), tile(POOL_WIDTH),
                  pl.BlockSpec((bb, POOL_PREV_ROWS, POOL_WIDTH), prev_map),
                  pl.BlockSpec(wpool_bf.shape, lambda bi, i: (0, 0, 0)),
                  const2((1, POOL_WIDTH)), const2(wout_bf.shape), const2((1, d)), const2((1, d))],
        out_specs=tile(d),
        out_shape=jax.ShapeDtypeStruct((b, s, d), F32),
        compiler_params=_compiler_params(("parallel", "arbitrary")),
        name="mix_ln1",
    )(x, attn, u, prev, wpool_bf, pscale, wout_bf, ln_g, ln_b)


def _sorting_network(n):
    pairs = []

    def merge(lo, m, r):
        step = 2 * r
        if step < m:
            merge(lo, m, step)
            merge(lo + r, m, step)
            pairs.extend((i, i + r) for i in range(lo + r, lo + m - r, step))
        else:
            pairs.append((lo, lo + r))

    def sort(lo, m):
        if m > 1:
            sort(lo, m // 2)
            sort(lo + m // 2, m // 2)
            merge(lo, m, 1)

    sort(0, n)
    return pairs


def _top_values(s, count):
    n = s.shape[0] // SUBLANES
    v = [s[SUBLANES * k:SUBLANES * (k + 1), :] for k in range(n)]
    for i, j in _sorting_network(n):
        v[i], v[j] = jnp.maximum(v[i], v[j]), jnp.minimum(v[i], v[j])
    ridx = lax.broadcasted_iota(jnp.int32, (count, s.shape[1]), 0)
    top = jnp.full((count, s.shape[1]), NEG_BIG, F32)
    for r in range(count):
        m = jnp.max(v[0], axis=0, keepdims=True)
        top = jnp.where(ridx == r, m, top)
        hit = v[0] == m
        need = count - r - 1
        for k in range(min(need, n - 1)):
            v[k] = jnp.where(hit, v[k + 1], v[k])
        if need >= n:
            v[n - 1] = jnp.where(hit, NEG_BIG, v[n - 1])
    return top


_PACKED_PAIRS = [(k, kp) for k in range(2, PEER_TOPK) for kp in range(PEER_TOPK // (k + 1))]


def _candidate_sums(a, b):
    t = a.shape[1]
    row = lax.broadcasted_iota(jnp.int32, (SUBLANES, t), 0)
    groups = [a[0:1, :] + b, a[1:2, :] + b[0:SUBLANES, :]]
    for g in range(0, len(_PACKED_PAIRS), SUBLANES):
        a_pat = jnp.full((SUBLANES, t), NEG_BIG, F32)
        b_pat = jnp.zeros((SUBLANES, t), F32)
        for r, (k, kp) in enumerate(_PACKED_PAIRS[g:g + SUBLANES]):
            a_pat = jnp.where(row == r, a[k:k + 1, :], a_pat)
            b_pat = jnp.where(row == r, b[kp:kp + 1, :], b_pat)
        groups.append(a_pat + b_pat)
    n_groups = sum(g.shape[0] for g in groups) // SUBLANES
    pad = (1 << (n_groups - 1).bit_length()) - n_groups
    groups += [jnp.full((SUBLANES, t), NEG_BIG, F32)] * pad
    return jnp.concatenate(groups, axis=0)


def _route_tile(s1, s2):
    k_top = PEER_TOPK
    a = _top_values(s1, k_top)
    b = _top_values(s2, k_top)
    cv = _top_values(_candidate_sums(a, b), k_top)
    least = cv[k_top - 1:k_top, :]
    z = jnp.sum(jnp.exp(cv - cv[0:1, :]), axis=0, keepdims=True)
    rank2 = jnp.full_like(s2, float(k_top))
    for k in reversed(range(k_top)):
        rank2 = jnp.where(s2 >= b[k:k + 1, :], float(k), rank2)
    cnt = jnp.zeros_like(s1)
    for k in range(SUBLANES):
        cnt = jnp.where(s1 + b[k:k + 1, :] >= least, float(k + 1), cnt)
    tail = jnp.sum(jnp.where(a[0:1, :] + b[SUBLANES:, :] >= least, 1.0, 0.0), axis=0, keepdims=True)
    cnt = cnt + jnp.where(s1 == a[0:1, :], tail, 0.0)
    return rank2, jnp.exp(s2 - b[0:1, :]), cnt, jnp.exp(s1 - a[0:1, :]) / z


def _score_weights_kernel(keys_ref, wq_ref, o_ref):
    o_ref[...] = lax.dot_general(keys_ref[0], wq_ref[...], _NT, precision=lax.Precision.HIGHEST,
                                 preferred_element_type=F32).astype(o_ref.dtype)


def _score_weights(keys, w_query):
    n_hp, n_keys, half = keys.shape
    d = w_query.shape[0]
    return pl.pallas_call(
        _score_weights_kernel,
        grid=(n_hp,),
        in_specs=[pl.BlockSpec((1, n_keys, half), lambda i: (i, 0, 0)),
                  pl.BlockSpec((d, half), lambda i: (0, i))],
        out_specs=pl.BlockSpec((n_keys, d), lambda i: (i, 0)),
        out_shape=jax.ShapeDtypeStruct((n_hp * n_keys, d), BF16),
        compiler_params=_compiler_params(("parallel",)),
        name="peer_score_weights",
    )(keys, w_query)


def _peer_route_kernel(x0_ref, xn_ref, w_ref, rk_ref, eb_ref, cnt_ref, ea_ref, s_ref):
    i = pl.program_id(0)

    def scores(x_ref, slot):
        s_ref[slot] = lax.dot_general(w_ref[...], x_ref[...].astype(BF16), _NT, preferred_element_type=F32)

    @pl.when(i == 0)
    def _():
        scores(x0_ref, 0)

    for cur in range(2):
        @pl.when(i % 2 == cur)
        def _():
            scores(xn_ref, 1 - cur)
            for h in range(PEER_HEADS):
                r1 = 2 * h * PEER_N_KEYS
                r2 = r1 + PEER_N_KEYS
                for lt in range(rk_ref.shape[2] // LANES):
                    sl = slice(lt * LANES, (lt + 1) * LANES)
                    rank2, eb, cnt, ea = _route_tile(s_ref[cur, r1:r2, sl], s_ref[cur, r2:r2 + PEER_N_KEYS, sl])
                    rk_ref[h, :, sl] = rank2.astype(BF16)
                    eb_ref[h, :, sl] = eb.astype(BF16)
                    cnt_ref[h, :, sl] = cnt
                    ea_ref[h, :, sl] = ea


def _peer_route(x1, w_scores, tr):
    n, d = x1.shape
    assert n % tr == 0 and tr % LANES == 0
    steps = n // tr
    out_block = pl.BlockSpec((PEER_HEADS, PEER_N_KEYS, tr), lambda i: (0, 0, i))
    shape = (PEER_HEADS, PEER_N_KEYS, n)
    return pl.pallas_call(
        _peer_route_kernel,
        grid=(steps,),
        in_specs=[pl.BlockSpec((tr, d), lambda i: (0, 0)),
                  pl.BlockSpec((tr, d), lambda i: (jnp.minimum(i + 1, steps - 1), 0)),
                  pl.BlockSpec(w_scores.shape, lambda i: (0, 0))],
        out_specs=[out_block] * 4,
        out_shape=[jax.ShapeDtypeStruct(shape, BF16), jax.ShapeDtypeStruct(shape, BF16),
                   jax.ShapeDtypeStruct(shape, F32), jax.ShapeDtypeStruct(shape, F32)],
        scratch_shapes=[pltpu.VMEM((2, w_scores.shape[0], tr), F32)],
        compiler_params=_compiler_params(("arbitrary",)),
        name="peer_route",
    )(x1, x1, w_scores)


def _peer_chunk(act_ref, i0, vt_ref, rk_ref, eb_ref, cnt_ref, ea_ref, rows_per_chunk):
    total = None
    for pair in range(rows_per_chunk // 2):
        coef = []
        for ii in (2 * pair, 2 * pair + 1):
            gate = None
            for h in range(PEER_HEADS):
                cnt_i = cnt_ref[h, pl.ds(i0 + ii, 1), :].astype(BF16)
                ea_i = ea_ref[h, pl.ds(i0 + ii, 1), :].astype(BF16)
                eb = eb_ref[h]
                term = ea_i * jnp.where(rk_ref[h] < cnt_i, eb, jnp.zeros_like(eb))
                gate = term if gate is None else gate + term
            a = act_ref[ii * PEER_N_KEYS:(ii + 1) * PEER_N_KEYS, :]
            gelu = 0.5 * a * (1.0 + lax.erf(a * (2.0 ** -0.5)))
            coef.append(gate * gelu.astype(BF16))
        lo = 2 * pair * PEER_N_KEYS
        part = jnp.dot(vt_ref[:, lo:lo + 2 * PEER_N_KEYS], jnp.concatenate(coef, axis=0),
                       preferred_element_type=F32)
        total = part if total is None else total + part
    return total


def _peer_kernel(x_ref, u0_ref, un_ref, vt_ref, rk_ref, eb_ref, cnt_ref, ea_ref, g_ref, b_ref, o_ref,
                 xb_ref, act_ref, acc_ref, *, alpha, rows_per_chunk):
    c = pl.program_id(1)
    route = (rk_ref, eb_ref, cnt_ref, ea_ref)

    @pl.when(c == 0)
    def _():
        xb_ref[...] = x_ref[...].astype(BF16)
        act_ref[0] = lax.dot_general(u0_ref[...], xb_ref[...], _NT, preferred_element_type=F32)
        acc_ref[...] = jnp.zeros_like(acc_ref)

    i0 = pl.multiple_of(c * rows_per_chunk, rows_per_chunk)
    for cur in range(2):
        @pl.when(c % 2 == cur)
        def _():
            act_ref[1 - cur] = lax.dot_general(un_ref[...], xb_ref[...], _NT, preferred_element_type=F32)
            acc_ref[...] += _peer_chunk(act_ref.at[cur], i0, vt_ref, *route, rows_per_chunk)

    @pl.when(c == pl.num_programs(1) - 1)
    def _():
        peer = acc_ref[...].T
        o_ref[...] = _layer_norm(alpha * x_ref[...] + peer, g_ref[...], b_ref[...])


def _peer(x1, w_scores, u_bf, vt_bf, ln_g, ln_b, *, tr, tm, rows_per_chunk, alpha):
    n, d = x1.shape
    n_exp = u_bf.shape[0]
    chunk = rows_per_chunk * PEER_N_KEYS
    assert n % tm == 0 and n_exp % (2 * chunk) == 0 and rows_per_chunk % SUBLANES == 0
    n_chunks = n_exp // chunk
    routed = _peer_route(x1, w_scores, tr)
    const = lambda shape: pl.BlockSpec(shape, lambda i, c: (0,) * len(shape))
    route_block = pl.BlockSpec((PEER_HEADS, PEER_N_KEYS, tm), lambda i, c: (0, 0, i))
    return pl.pallas_call(
        functools.partial(_peer_kernel, alpha=alpha, rows_per_chunk=rows_per_chunk),
        grid=(n // tm, n_chunks),
        in_specs=[pl.BlockSpec((tm, d), lambda i, c: (i, 0)),
                  pl.BlockSpec((chunk, d), lambda i, c: (0, 0)),
                  pl.BlockSpec((chunk, d), lambda i, c: (jnp.minimum(c + 1, n_chunks - 1), 0)),
                  pl.BlockSpec((d, chunk), lambda i, c: (0, c)),
                  route_block, route_block, route_block, route_block,
                  const((1, d)), const((1, d))],
        out_specs=pl.BlockSpec((tm, d), lambda i, c: (i, 0)),
        out_shape=jax.ShapeDtypeStruct((n, d), F32),
        scratch_shapes=[pltpu.VMEM((tm, d), BF16), pltpu.VMEM((2, chunk, tm), F32), pltpu.VMEM((d, tm), F32)],
        compiler_params=_compiler_params(("parallel", "arbitrary")),
        name="peer_ln2",
    )(x1, u_bf, u_bf, vt_bf, *routed, ln_g, ln_b)


def _pick(n, prefs):
    for p in prefs:
        if n % p == 0:
            return p
    return n


def kernel(x_prompt, x_sample, cache_k, cache_v, state_pool, page_table, w_in, sb_bias, w_out, w_pool,
           pool_scale, ln1_g, ln1_b, w_query, sub_keys, u_table, v_table, ln2_g, ln2_b):
    depth = w_in.shape[0]
    b, s, d = x_prompt.shape
    db, t_new, _ = x_sample.shape
    n_pages = page_table.shape[1]
    past_len = n_pages * PAGE_SIZE
    alpha = (2.0 * depth) ** 0.25
    n_pool = cache_k.shape[1]

    hp, hs = x_prompt, x_sample
    outs = [[] for _ in range(6)]
    for l in range(depth):
        w_in_bf = w_in[l].astype(BF16)
        wkvt_bf = w_in[l][:, SB_WIDTH:3 * SB_WIDTH].T.astype(BF16)
        wout_bf = w_out[l].astype(BF16)
        wpool_bf = w_pool[l].astype(BF16)
        w_scores = _score_weights(sub_keys[l].reshape(2 * PEER_HEADS, PEER_N_KEYS, PEER_HALF), w_query[l])
        u_bf = u_table[l].astype(BF16)
        vt_bf = v_table[l].T.astype(BF16)
        pscale = pool_scale[l].reshape(1, POOL_WIDTH)
        g1, b1 = ln1_g[l].reshape(1, d), ln1_b[l].reshape(1, d)
        g2, b2 = ln2_g[l].reshape(1, d), ln2_b[l].reshape(1, d)
        bias = sb_bias[l].astype(F32)
        peer = functools.partial(_peer, w_scores=w_scores, u_bf=u_bf, vt_bf=vt_bf,
                                 ln_g=g2, ln_b=b2, rows_per_chunk=SUBLANES, alpha=alpha)

        n = b * s
        q, kt, vt, u, kb, vb = _in_proj(hp.reshape(n, d), w_in_bf, wkvt_bf, _pick(s, (512, 256, 128)), seq=s)
        attn = _sb_prompt(bias, q.reshape(b, s, SB_WIDTH), kb.reshape(b, s, SB_WIDTH),
                          vb.reshape(b, s, SB_WIDTH), _pick(s, (256, 128)))
        u3 = u.reshape(b, s, POOL_WIDTH)
        x1 = _mix(hp, attn, u3, u3, wpool_bf, pscale, wout_bf, g1, b1,
                  bb=1, t=_pick(s, (512, 256, 128)), alpha=alpha, pos0=0, zero_first_prev=True)
        hp = peer(x1.reshape(n, d), tr=_pick(n, (256, 128)), tm=_pick(n, (512, 256, 128))).reshape(b, s, d)
        outs[0].append(kt.reshape(b, SB_HEADS, SB_HEAD_DIM, s).transpose(0, 3, 1, 2))
        outs[1].append(vt.reshape(b, SB_HEADS, SB_HEAD_DIM, s).transpose(0, 3, 1, 2))
        outs[2].append(u3[:, s - POOL_STATE_LEN:])

        ns = db * t_new
        qs, ks, vs, us, _, _ = _in_proj(hs.reshape(ns, d), w_in_bf, wkvt_bf, _pick(ns, (512, 256, 128)))
        attn_s = _sb_sample(page_table, bias, qs.reshape(db, t_new, SB_WIDTH),
                            ks.reshape(db, t_new, SB_WIDTH), vs.reshape(db, t_new, SB_WIDTH),
                            cache_k[l].transpose(0, 2, 3, 1), cache_v[l].transpose(0, 2, 3, 1),
                            _pick(n_pages, (8, 4, 2, 1)))
        us3 = us.reshape(db, t_new, POOL_WIDTH)
        state = state_pool[l].astype(F32)
        prev = jnp.pad(state, ((0, 0), (POOL_PREV_ROWS - POOL_STATE_LEN, 0), (0, 0)))
        x1s = _mix(hs, attn_s, us3, prev, wpool_bf, pscale, wout_bf, g1, b1,
                   bb=_pick(db, (16, 8, 4, 2, 1)), t=t_new, alpha=alpha, pos0=past_len, zero_first_prev=False)
        hs = peer(x1s.reshape(ns, d), tr=_pick(ns, (256, 128)), tm=_pick(ns, (512, 256, 128))).reshape(db, t_new, d)
        outs[3].append(ks.reshape(db, t_new, SB_HEADS, SB_HEAD_DIM))
        outs[4].append(vs.reshape(db, t_new, SB_HEADS, SB_HEAD_DIM))
        outs[5].append(jnp.concatenate([state, us3], axis=1)[:, -POOL_STATE_LEN:])

    return (hp, hs, jnp.stack(outs[0]), jnp.stack(outs[1]), jnp.stack(outs[2]),
            jnp.stack(outs[3]), jnp.stack(outs[4]), jnp.stack(outs[5]))
```

```python
import functools
import math

import jax
import jax.numpy as jnp
from jax import lax
from jax.experimental import pallas as pl
from jax.experimental.pallas import tpu as pltpu

F32 = jnp.float32
BF16 = jnp.bfloat16

SB_HEADS = 8
SB_HEAD_DIM = 64
SB_WIDTH = SB_HEADS * SB_HEAD_DIM
POOL_WINDOWS = (2, 4, 8, 16)
POOL_GROUP_DIM = 128
POOL_WIDTH = len(POOL_WINDOWS) * POOL_GROUP_DIM
POOL_STATE_LEN = max(POOL_WINDOWS) - 1
POOL_PREV_ROWS = 16
PAGE_SIZE = 128
PEER_HEADS = 8
PEER_N_KEYS = 128
PEER_TOPK = 16
PEER_HALF = 128
PEER_CHUNK_ROWS = 8
LN_EPS = 1e-5

LANES = 128
SUBLANES = 8
BF16_TILE_ROWS = 16
VMEM_LIMIT_BYTES = 56 * 1024 * 1024

NEG_BIG = -1e30
LOG2_E = 1.4426950408889634
_NT = (((1,), (1,)), ((), ()))


def _compiler_params(semantics):
    return pltpu.CompilerParams(dimension_semantics=semantics, vmem_limit_bytes=VMEM_LIMIT_BYTES)


def _proj_kernel(x_ref, w_ref, wkvt_ref, q_ref, k_ref, v_ref, u_ref, kb_ref, vb_ref, *, tokens_last):
    xb = x_ref[...].astype(BF16)
    w = SB_WIDTH
    q = jnp.dot(xb, w_ref[:, 0:w], preferred_element_type=F32)
    q_ref[...] = (q * -(SB_HEAD_DIM ** -0.5)).astype(BF16)
    k = jnp.dot(xb, w_ref[:, w:2 * w], preferred_element_type=F32)
    kb_ref[...] = k.astype(BF16)
    v = jnp.dot(xb, w_ref[:, 2 * w:3 * w], preferred_element_type=F32)
    vb_ref[...] = v.astype(BF16)
    u_ref[...] = jnp.dot(xb, w_ref[:, 3 * w:], preferred_element_type=F32)
    if tokens_last:
        k_ref[0] = lax.dot_general(wkvt_ref[0:w, :], xb, _NT, preferred_element_type=F32)
        v_ref[0] = lax.dot_general(wkvt_ref[w:2 * w, :], xb, _NT, preferred_element_type=F32)
    else:
        k_ref[...] = k
        v_ref[...] = v


def _in_proj(x2d, w_in_bf, wkvt_bf, tm, seq=None):
    n, d = x2d.shape
    pw = w_in_bf.shape[1]
    row = lambda i: (i, 0)
    out_block = pl.BlockSpec((tm, SB_WIDTH), row)
    if seq is None:
        kv_block, kv_shape = out_block, (n, SB_WIDTH)
    else:
        assert seq % tm == 0 and n % seq == 0
        per_seq = seq // tm
        kv_block = pl.BlockSpec((1, SB_WIDTH, tm), lambda i: (i // per_seq, 0, i % per_seq))
        kv_shape = (n // seq, SB_WIDTH, seq)
    return pl.pallas_call(
        functools.partial(_proj_kernel, tokens_last=seq is not None),
        grid=(n // tm,),
        in_specs=[pl.BlockSpec((tm, d), row), pl.BlockSpec((d, pw), lambda i: (0, 0)),
                  pl.BlockSpec(wkvt_bf.shape, lambda i: (0, 0))],
        out_specs=[out_block, kv_block, kv_block, out_block, out_block, out_block],
        out_shape=[jax.ShapeDtypeStruct((n, SB_WIDTH), BF16),
                   jax.ShapeDtypeStruct(kv_shape, F32),
                   jax.ShapeDtypeStruct(kv_shape, F32),
                   jax.ShapeDtypeStruct((n, POOL_WIDTH), F32),
                   jax.ShapeDtypeStruct((n, SB_WIDTH), BF16),
                   jax.ShapeDtypeStruct((n, SB_WIDTH), BF16)],
        compiler_params=_compiler_params(("parallel",)),
        name="in_proj",
    )(x2d, w_in_bf, wkvt_bf)


def _suffix_sum_matrix(tk):
    r = lax.broadcasted_iota(jnp.int32, (tk, 2 * tk), 0)
    c = lax.broadcasted_iota(jnp.int32, (tk, 2 * tk), 1)
    return jnp.where((r > c) | (c >= tk), 1.0, 0.0).astype(BF16)


def _log_one_minus_beta(nz):
    neg_abs = lax.bitcast_convert_type(lax.bitcast_convert_type(nz, jnp.uint32) | jnp.uint32(1 << 31), F32)
    return jnp.minimum(nz, 0.0) - jnp.log(1.0 + jnp.exp2(neg_abs * LOG2_E))


def _sb_weights(nzs, carry, tmat, causal):
    rows, tk = nzs[0].shape
    log1m = [_log_one_minus_beta(nz) for nz in nzs]
    if causal is not None:
        log1m = [jnp.where(causal, l, 0.0) for l in log1m]
    sums = jnp.dot(jnp.concatenate(log1m, axis=0).astype(BF16), tmat, preferred_element_type=F32)
    ws = []
    for p, (nz, l) in enumerate(zip(nzs, log1m)):
        sp = sums[p * rows:(p + 1) * rows]
        w = jnp.exp((l - nz) + (sp[:, :tk] + carry))
        if causal is not None:
            w = jnp.where(causal, w, 0.0)
        ws.append(w.astype(BF16))
        carry = carry + sp[:, tk:]
    return ws, carry


def _sb_prompt_kernel(bias_ref, q_ref, k_ref, v_ref, o_ref,
                      qs_ref, zl_ref, l_ref, w_ref, carry_ref, acc_ref, *, tq):
    qi = pl.program_id(1)
    n_pairs = SB_HEADS // 2
    r = lax.broadcasted_iota(jnp.int32, (tq, tq), 0)
    c = lax.broadcasted_iota(jnp.int32, (tq, tq), 1)
    strict = c < r
    tmat = jnp.where(r > c, 1.0, 0.0).astype(BF16)
    lane = lax.broadcasted_iota(jnp.int32, (tq, LANES), 1)
    low = lane < SB_HEAD_DIM
    for hp in range(n_pairs):
        q = q_ref[0, :, hp * LANES:(hp + 1) * LANES]
        zero = jnp.zeros_like(q)
        qs_ref[hp, 0:tq, :] = jnp.where(low, q, zero)
        qs_ref[hp, tq:2 * tq, :] = jnp.where(low, zero, q)
    carry_ref[...] = jnp.zeros_like(carry_ref)
    acc_ref[...] = jnp.zeros_like(acc_ref)

    def block(j, causal):
        start = pl.multiple_of(j * tq, tq)
        for hp in range(n_pairs):
            kj = k_ref[0, pl.ds(start, tq), hp * LANES:(hp + 1) * LANES]
            z2 = lax.dot_general(qs_ref[hp], kj, (((1,), (1,)), ((), ())), preferred_element_type=F32)
            for h2 in range(2):
                h = 2 * hp + h2
                nz = z2[h2 * tq:(h2 + 1) * tq, :] - bias_ref[h]
                log1m = _log_one_minus_beta(nz)
                if causal:
                    log1m = jnp.where(strict, log1m, 0.0)
                zl_ref[h] = log1m - nz
                l_ref[h * tq:(h + 1) * tq, :] = log1m.astype(BF16)
        later = jnp.dot(l_ref[...], tmat, preferred_element_type=F32)
        for hp in range(n_pairs):
            for h2 in range(2):
                h = 2 * hp + h2
                rest = later[h * tq:(h + 1) * tq, :]
                first = l_ref[h * tq:(h + 1) * tq, 0:1].astype(F32)
                rowsum = jnp.broadcast_to(rest[:, 0:1] + first, (tq, LANES))
                carry = carry_ref[h]
                w = jnp.exp(zl_ref[h] + rest + jnp.concatenate([carry] * (tq // LANES), axis=1))
                if causal:
                    w = jnp.where(strict, w, 0.0)
                w_ref[hp, h2 * tq:(h2 + 1) * tq, :] = w.astype(BF16)
                carry_ref[h] = carry + rowsum
        for hp in range(n_pairs):
            vj = v_ref[0, pl.ds(start, tq), hp * LANES:(hp + 1) * LANES]
            acc_ref[hp] += jnp.dot(w_ref[hp], vj, preferred_element_type=F32)

    block(qi, True)

    def body(it, _):
        block(qi - 1 - it, False)
        return 0

    lax.fori_loop(0, qi, body, 0)
    for hp in range(n_pairs):
        o_ref[0, :, hp * LANES:(hp + 1) * LANES] = jnp.where(
            low, acc_ref[hp, 0:tq, :], acc_ref[hp, tq:2 * tq, :]).astype(o_ref.dtype)


def _sb_prompt(bias, q, kb, vb, tq):
    b, s, _ = q.shape
    assert tq % LANES == 0 and s % tq == 0
    qspec = pl.BlockSpec((1, tq, SB_WIDTH), lambda bi, qi: (bi, qi, 0))
    kvspec = pl.BlockSpec((1, s, SB_WIDTH), lambda bi, qi: (bi, 0, 0))
    n_pairs = SB_HEADS // 2
    return pl.pallas_call(
        functools.partial(_sb_prompt_kernel, tq=tq),
        grid=(b, s // tq),
        in_specs=[pl.BlockSpec(memory_space=pltpu.SMEM), qspec, kvspec, kvspec],
        out_specs=qspec,
        out_shape=jax.ShapeDtypeStruct((b, s, SB_WIDTH), BF16),
        scratch_shapes=[pltpu.VMEM((n_pairs, 2 * tq, LANES), BF16),
                        pltpu.VMEM((SB_HEADS, tq, tq), F32),
                        pltpu.VMEM((SB_HEADS * tq, tq), BF16),
                        pltpu.VMEM((n_pairs, 2 * tq, tq), BF16),
                        pltpu.VMEM((SB_HEADS, tq, LANES), F32),
                        pltpu.VMEM((n_pairs, 2 * tq, LANES), F32)],
        compiler_params=_compiler_params(("parallel", "arbitrary")),
        name="sb_prompt",
    )(bias, q, kb, vb)


def _sb_sample_kernel(pt_ref, bias_ref, q_ref, kn_ref, vn_ref, *rest, pages_per_step, t_new):
    del pt_ref
    kpages = rest[:pages_per_step]
    vpages = rest[pages_per_step:2 * pages_per_step]
    o_ref = rest[2 * pages_per_step]
    qbd_ref, biasrow_ref, carry_ref, acc_ref = rest[2 * pages_per_step + 1:]
    g = pl.program_id(1)
    rows = SB_HEADS * t_new
    tmat = _suffix_sum_matrix(PAGE_SIZE)

    def weights(zs, causal):
        ws, c = _sb_weights([z - biasrow_ref[...] for z in zs], carry_ref[...], tmat, causal)
        carry_ref[...] = c
        return ws

    @pl.when(g == 0)
    def _():
        r = lax.broadcasted_iota(jnp.int32, (rows, SB_WIDTH), 0)
        c = lax.broadcasted_iota(jnp.int32, (rows, SB_WIDTH), 1)
        qrep = jnp.concatenate([q_ref[0].astype(F32)] * SB_HEADS, axis=0)
        qbd_ref[...] = jnp.where(c // SB_HEAD_DIM == r // t_new, qrep, 0.0).astype(BF16)
        rr = lax.broadcasted_iota(jnp.int32, (rows, PAGE_SIZE), 0)
        brow = jnp.zeros((rows, PAGE_SIZE), F32)
        for h in range(SB_HEADS):
            brow = jnp.where(rr // t_new == h, bias_ref[h], brow)
        biasrow_ref[...] = brow
        carry_ref[...] = jnp.zeros_like(carry_ref)
        acc_ref[...] = jnp.zeros_like(acc_ref)
        pad = jnp.zeros((PAGE_SIZE - t_new, SB_WIDTH), F32)
        kn = jnp.concatenate([kn_ref[0], pad], axis=0).astype(BF16)
        vn = jnp.concatenate([vn_ref[0], pad], axis=0).astype(BF16)
        cc = lax.broadcasted_iota(jnp.int32, (rows, PAGE_SIZE), 1)
        w_new, = weights([lax.dot_general(qbd_ref[...], kn, _NT, preferred_element_type=F32)],
                         cc < rr % t_new)
        acc_ref[...] += jnp.dot(w_new, vn, preferred_element_type=F32)

    qbd = qbd_ref[...]
    ws = weights([jnp.dot(qbd, kp[0].reshape(SB_WIDTH, PAGE_SIZE).astype(BF16), preferred_element_type=F32)
                  for kp in kpages], None)
    vt = jnp.concatenate([vp[0].reshape(SB_WIDTH, PAGE_SIZE).astype(BF16) for vp in vpages], axis=1)
    acc_ref[...] += lax.dot_general(jnp.concatenate(ws, axis=1), vt, _NT, preferred_element_type=F32)

    @pl.when(g == pl.num_programs(1) - 1)
    def _():
        c = lax.broadcasted_iota(jnp.int32, (t_new, SB_WIDTH), 1)
        out = jnp.zeros((t_new, SB_WIDTH), F32)
        for h in range(SB_HEADS):
            out = jnp.where(c // SB_HEAD_DIM == h, acc_ref[h * t_new:(h + 1) * t_new, :], out)
        o_ref[0] = out.astype(o_ref.dtype)


def _sb_sample(page_table, bias, q, k_new, v_new, cache_k, cache_v, pages_per_step):
    b, t_new, _ = q.shape
    n_pages = page_table.shape[1]
    assert n_pages % pages_per_step == 0 and t_new % SUBLANES == 0
    steps = n_pages // pages_per_step
    rows = SB_HEADS * t_new

    def page_spec(p):
        return pl.BlockSpec((1, SB_HEADS, SB_HEAD_DIM, PAGE_SIZE),
                            lambda bi, g, pt: (pt[bi, n_pages - 1 - (g * pages_per_step + p)], 0, 0, 0))

    tok = lambda dt: pl.BlockSpec((1, t_new, SB_WIDTH), lambda bi, g, pt: (bi, 0, 0))
    grid_spec = pltpu.PrefetchScalarGridSpec(
        num_scalar_prefetch=1,
        grid=(b, steps),
        in_specs=[pl.BlockSpec(memory_space=pltpu.SMEM), tok(BF16), tok(F32), tok(F32)]
        + [page_spec(p) for p in range(pages_per_step)] * 2,
        out_specs=pl.BlockSpec((1, t_new, SB_WIDTH), lambda bi, g, pt: (bi, 0, 0)),
        scratch_shapes=[pltpu.VMEM((rows, SB_WIDTH), BF16),
                        pltpu.VMEM((rows, PAGE_SIZE), F32),
                        pltpu.VMEM((rows, PAGE_SIZE), F32),
                        pltpu.VMEM((rows, SB_WIDTH), F32)],
    )
    return pl.pallas_call(
        functools.partial(_sb_sample_kernel, pages_per_step=pages_per_step, t_new=t_new),
        grid_spec=grid_spec,
        out_shape=jax.ShapeDtypeStruct((b, t_new, SB_WIDTH), F32),
        compiler_params=_compiler_params(("parallel", "arbitrary")),
        name="sb_sample",
    )(page_table, bias, q, k_new, v_new,
      *([cache_k] * pages_per_step), *([cache_v] * pages_per_step))


def _layer_norm(x, g, b):
    mu = jnp.mean(x, axis=-1, keepdims=True)
    xc = x - mu
    var = jnp.mean(xc * xc, axis=-1, keepdims=True)
    return xc * lax.rsqrt(var + LN_EPS) * g + b


def _mix_kernel(x_ref, attn_ref, u_ref, prev_ref, wpool_ref, pscale_ref, wout_ref, g_ref, b_ref, o_ref,
                *, alpha, pos0, zero_first_prev):
    bb, t, _ = u_ref.shape
    i = pl.program_id(1)
    prev = prev_ref[...]
    if zero_first_prev:
        prev = jnp.where(i == 0, 0.0, prev)
    ext = jnp.concatenate([prev, u_ref[...]], axis=1)
    pos = pos0 + i * t + lax.broadcasted_iota(jnp.int32, (1, t, 1), 1)
    mixed = []
    for gi, win in enumerate(POOL_WINDOWS):
        sl = slice(gi * POOL_GROUP_DIM, (gi + 1) * POOL_GROUP_DIM)
        acc = ext[:, :, sl]
        span = 1
        while span < win:
            acc = acc[:, span:, :] + acc[:, :acc.shape[1] - span, :]
            span *= 2
        tok = ext[:, POOL_PREV_ROWS:, sl]
        inv = 1.0 / jnp.minimum(win, pos + 1).astype(F32)
        pooled = acc[:, acc.shape[1] - t:, :] * inv - tok
        pooled = pooled.reshape(bb * t, POOL_GROUP_DIM).astype(BF16)
        mixed.append(jnp.dot(pooled, wpool_ref[gi], preferred_element_type=F32))
    pool_out = (jnp.concatenate(mixed, axis=-1) * pscale_ref[...]).astype(BF16)
    attn = attn_ref[...].reshape(bb * t, SB_WIDTH).astype(BF16)
    mix = (jnp.dot(attn, wout_ref[0:SB_WIDTH, :], preferred_element_type=F32)
           + jnp.dot(pool_out, wout_ref[SB_WIDTH:, :], preferred_element_type=F32))
    x = x_ref[...].reshape(bb * t, x_ref.shape[2])
    o_ref[...] = _layer_norm(alpha * x + mix, g_ref[...], b_ref[...]).reshape(o_ref.shape)


def _mix(x, attn, u, prev, wpool_bf, pscale, wout_bf, ln_g, ln_b, *, bb, t, alpha, pos0, zero_first_prev):
    b, s, d = x.shape
    assert b % bb == 0 and s % t == 0 and t % SUBLANES == 0
    if zero_first_prev:
        assert t % POOL_PREV_ROWS == 0
        prev_map = lambda bi, i: (bi, jnp.maximum(i * (t // POOL_PREV_ROWS) - 1, 0), 0)
    else:
        assert s == t
        prev_map = lambda bi, i: (bi, 0, 0)
    tile = lambda w: pl.BlockSpec((bb, t, w), lambda bi, i: (bi, i, 0))
    const2 = lambda shape: pl.BlockSpec(shape, lambda bi, i: (0, 0))
    return pl.pallas_call(
        functools.partial(_mix_kernel, alpha=alpha, pos0=pos0, zero_first_prev=zero_first_prev),
        grid=(b // bb, s // t),
        in_specs=[tile(d), tile(SB_WIDTH), tile(POOL_WIDTH),
                  pl.BlockSpec((bb, POOL_PREV_ROWS, POOL_WIDTH), prev_map),
                  pl.BlockSpec(wpool_bf.shape, lambda bi, i: (0, 0, 0)),
                  const2((1, POOL_WIDTH)), const2(wout_bf.shape), const2((1, d)), const2((1, d))],
        out_specs=tile(d),
        out_shape=jax.ShapeDtypeStruct((b, s, d), F32),
        compiler_params=_compiler_params(("parallel", "arbitrary")),
        name="mix_ln1",
    )(x, attn, u, prev, wpool_bf, pscale, wout_bf, ln_g, ln_b)


def _sorting_network(n):
    pairs = []

    def merge(lo, m, r):
        step = 2 * r
        if step < m:
            merge(lo, m, step)
            merge(lo + r, m, step)
            pairs.extend((i, i + r) for i in range(lo + r, lo + m - r, step))
        else:
            pairs.append((lo, lo + r))

    def sort(lo, m):
        if m > 1:
            sort(lo, m // 2)
            sort(lo + m // 2, m // 2)
            merge(lo, m, 1)

    sort(0, n)
    return pairs


def _top_values(s, count):
    n = s.shape[0] // SUBLANES
    v = [s[SUBLANES * k:SUBLANES * (k + 1), :] for k in range(n)]
    for i, j in _sorting_network(n):
        v[i], v[j] = jnp.maximum(v[i], v[j]), jnp.minimum(v[i], v[j])
    ridx = lax.broadcasted_iota(jnp.int32, (count, s.shape[1]), 0)
    top = jnp.full((count, s.shape[1]), NEG_BIG, F32)
    for r in range(count):
        m = jnp.max(v[0], axis=0, keepdims=True)
        top = jnp.where(ridx == r, m, top)
        hit = v[0] == m
        need = count - r - 1
        for k in range(min(need, n - 1)):
            v[k] = jnp.where(hit, v[k + 1], v[k])
        if need >= n:
            v[n - 1] = jnp.where(hit, NEG_BIG, v[n - 1])
    return top


_PACKED_PAIRS = [(k, kp) for k in range(2, PEER_TOPK) for kp in range(PEER_TOPK // (k + 1))]


def _candidate_sums(a, b):
    t = a.shape[1]
    row = lax.broadcasted_iota(jnp.int32, (SUBLANES, t), 0)
    groups = [a[0:1, :] + b, a[1:2, :] + b[0:SUBLANES, :]]
    for g in range(0, len(_PACKED_PAIRS), SUBLANES):
        a_pat = jnp.full((SUBLANES, t), NEG_BIG, F32)
        b_pat = jnp.zeros((SUBLANES, t), F32)
        for r, (k, kp) in enumerate(_PACKED_PAIRS[g:g + SUBLANES]):
            a_pat = jnp.where(row == r, a[k:k + 1, :], a_pat)
            b_pat = jnp.where(row == r, b[kp:kp + 1, :], b_pat)
        groups.append(a_pat + b_pat)
    n_groups = sum(g.shape[0] for g in groups) // SUBLANES
    pad = (1 << (n_groups - 1).bit_length()) - n_groups
    groups += [jnp.full((SUBLANES, t), NEG_BIG, F32)] * pad
    return jnp.concatenate(groups, axis=0)


def _route_tile(s1, s2):
    k_top = PEER_TOPK
    a = _top_values(s1, k_top)
    b = _top_values(s2, k_top)
    cv = _top_values(_candidate_sums(a, b), k_top)
    least = cv[k_top - 1:k_top, :]
    z = jnp.sum(jnp.exp(cv - cv[0:1, :]), axis=0, keepdims=True)
    rank2 = jnp.full_like(s2, float(k_top))
    for k in reversed(range(k_top)):
        rank2 = jnp.where(s2 >= b[k:k + 1, :], float(k), rank2)
    cnt = jnp.zeros_like(s1)
    for k in range(SUBLANES):
        cnt = jnp.where(s1 + b[k:k + 1, :] >= least, float(k + 1), cnt)
    tail = jnp.sum(jnp.where(a[0:1, :] + b[SUBLANES:, :] >= least, 1.0, 0.0), axis=0, keepdims=True)
    cnt = cnt + jnp.where(s1 == a[0:1, :], tail, 0.0)
    return rank2, jnp.exp(s2 - b[0:1, :]), cnt, jnp.exp(s1 - a[0:1, :]) / z


def _score_weights_kernel(keys_ref, wq_ref, o_ref):
    o_ref[...] = lax.dot_general(keys_ref[0], wq_ref[...], _NT, precision=lax.Precision.HIGHEST,
                                 preferred_element_type=F32).astype(o_ref.dtype)


def _score_weights(keys, w_query):
    n_hp, n_keys, half = keys.shape
    d = w_query.shape[0]
    return pl.pallas_call(
        _score_weights_kernel,
        grid=(n_hp,),
        in_specs=[pl.BlockSpec((1, n_keys, half), lambda i: (i, 0, 0)),
                  pl.BlockSpec((d, half), lambda i: (0, i))],
        out_specs=pl.BlockSpec((n_keys, d), lambda i: (i, 0)),
        out_shape=jax.ShapeDtypeStruct((n_hp * n_keys, d), BF16),
        compiler_params=_compiler_params(("parallel",)),
        name="peer_score_weights",
    )(keys, w_query)


def _peer_route_kernel(x0_ref, xn_ref, w_ref, rk_ref, eb_ref, cnt_ref, ea_ref, s_ref):
    i = pl.program_id(0)

    def scores(x_ref, slot):
        s_ref[slot] = lax.dot_general(w_ref[...], x_ref[...].astype(BF16), _NT, preferred_element_type=F32)

    @pl.when(i == 0)
    def _():
        scores(x0_ref, 0)

    for cur in range(2):
        @pl.when(i % 2 == cur)
        def _():
            scores(xn_ref, 1 - cur)
            for h in range(PEER_HEADS):
                r1 = 2 * h * PEER_N_KEYS
                r2 = r1 + PEER_N_KEYS
                for lt in range(rk_ref.shape[2] // LANES):
                    sl = slice(lt * LANES, (lt + 1) * LANES)
                    rank2, eb, cnt, ea = _route_tile(s_ref[cur, r1:r2, sl], s_ref[cur, r2:r2 + PEER_N_KEYS, sl])
                    rk_ref[h, :, sl] = rank2.astype(BF16)
                    eb_ref[h, :, sl] = eb.astype(BF16)
                    cnt_ref[h, :, sl] = cnt
                    ea_ref[h, :, sl] = ea


def _peer_route(x1, w_scores, tr):
    n, d = x1.shape
    assert n % tr == 0 and tr % LANES == 0
    steps = n // tr
    out_block = pl.BlockSpec((PEER_HEADS, PEER_N_KEYS, tr), lambda i: (0, 0, i))
    shape = (PEER_HEADS, PEER_N_KEYS, n)
    return pl.pallas_call(
        _peer_route_kernel,
        grid=(steps,),
        in_specs=[pl.BlockSpec((tr, d), lambda i: (0, 0)),
                  pl.BlockSpec((tr, d), lambda i: (jnp.minimum(i + 1, steps - 1), 0)),
                  pl.BlockSpec(w_scores.shape, lambda i: (0, 0))],
        out_specs=[out_block] * 4,
        out_shape=[jax.ShapeDtypeStruct(shape, BF16), jax.ShapeDtypeStruct(shape, BF16),
                   jax.ShapeDtypeStruct(shape, F32), jax.ShapeDtypeStruct(shape, F32)],
        scratch_shapes=[pltpu.VMEM((2, w_scores.shape[0], tr), F32)],
        compiler_params=_compiler_params(("arbitrary",)),
        name="peer_route",
    )(x1, x1, w_scores)


def _peer_chunk(act_ref, i0, vt_ref, rk_ref, eb_ref, cnt_ref, ea_ref, rows_per_chunk):
    tm = act_ref.shape[1]
    tile = (BF16_TILE_ROWS, tm)
    tiles = (PEER_N_KEYS // BF16_TILE_ROWS, BF16_TILE_ROWS, tm)
    total = None
    for pair in range(rows_per_chunk // 2):
        coef = []
        for ii in (2 * pair, 2 * pair + 1):
            gate = None
            for h in range(PEER_HEADS):
                cnt_i = jnp.broadcast_to(cnt_ref[h, pl.ds(i0 + ii, 1), :], tile).astype(BF16)
                ea_i = jnp.broadcast_to(ea_ref[h, pl.ds(i0 + ii, 1), :], tile).astype(BF16)
                eb = eb_ref[h].reshape(tiles)
                term = jnp.where(rk_ref[h].reshape(tiles) < cnt_i[None], ea_i[None] * eb, jnp.zeros_like(eb))
                gate = term if gate is None else gate + term
            a = act_ref[ii * PEER_N_KEYS:(ii + 1) * PEER_N_KEYS, :]
            gelu = 0.5 * a * (1.0 + lax.erf(a * (2.0 ** -0.5)))
            coef.append(gate.reshape(PEER_N_KEYS, tm) * gelu.astype(BF16))
        lo = 2 * pair * PEER_N_KEYS
        part = jnp.dot(vt_ref[0, :, lo:lo + 2 * PEER_N_KEYS], jnp.concatenate(coef, axis=0),
                       preferred_element_type=F32)
        total = part if total is None else total + part
    return total


def _peer_kernel(x_ref, u0_ref, un_ref, vt_ref, rk_ref, eb_ref, cnt_ref, ea_ref, g_ref, b_ref, o_ref,
                 xb_ref, act_ref, acc_ref, *, alpha, rows_per_chunk):
    c = pl.program_id(1)
    route = (rk_ref, eb_ref, cnt_ref, ea_ref)

    @pl.when(c == 0)
    def _():
        xb_ref[...] = x_ref[...].T.astype(BF16)
        act_ref[0] = jnp.dot(u0_ref[...], xb_ref[...], preferred_element_type=F32)
        acc_ref[...] = jnp.zeros_like(acc_ref)

    i0 = pl.multiple_of(c * rows_per_chunk, rows_per_chunk)
    for cur in range(2):
        @pl.when(c % 2 == cur)
        def _():
            act_ref[1 - cur] = jnp.dot(un_ref[...], xb_ref[...], preferred_element_type=F32)
            acc_ref[...] += _peer_chunk(act_ref.at[cur], i0, vt_ref, *route, rows_per_chunk)

    @pl.when(c == pl.num_programs(1) - 1)
    def _():
        peer = acc_ref[...].T
        o_ref[...] = _layer_norm(alpha * x_ref[...] + peer, g_ref[...], b_ref[...])


def _peer(x1, w_scores, u_bf, vt_bf, ln_g, ln_b, *, tr, tm, rows_per_chunk, alpha):
    n, d = x1.shape
    n_exp = u_bf.shape[0]
    chunk = rows_per_chunk * PEER_N_KEYS
    assert n % tm == 0 and n_exp % (2 * chunk) == 0 and rows_per_chunk % SUBLANES == 0
    n_chunks = n_exp // chunk
    routed = _peer_route(x1, w_scores, tr)
    const = lambda shape: pl.BlockSpec(shape, lambda i, c: (0,) * len(shape))
    route_block = pl.BlockSpec((PEER_HEADS, PEER_N_KEYS, tm), lambda i, c: (0, 0, i))
    return pl.pallas_call(
        functools.partial(_peer_kernel, alpha=alpha, rows_per_chunk=rows_per_chunk),
        grid=(n // tm, n_chunks),
        in_specs=[pl.BlockSpec((tm, d), lambda i, c: (i, 0)),
                  pl.BlockSpec((chunk, d), lambda i, c: (0, 0)),
                  pl.BlockSpec((chunk, d), lambda i, c: (jnp.minimum(c + 1, n_chunks - 1), 0)),
                  pl.BlockSpec((1, d, chunk), lambda i, c: (c, 0, 0)),
                  route_block, route_block, route_block, route_block,
                  const((1, d)), const((1, d))],
        out_specs=pl.BlockSpec((tm, d), lambda i, c: (i, 0)),
        out_shape=jax.ShapeDtypeStruct((n, d), F32),
        scratch_shapes=[pltpu.VMEM((d, tm), BF16), pltpu.VMEM((2, chunk, tm), F32), pltpu.VMEM((d, tm), F32)],
        compiler_params=_compiler_params(("parallel", "arbitrary")),
        name="peer_ln2",
    )(x1, u_bf, u_bf, vt_bf, *routed, ln_g, ln_b)


def _pick(n, prefs):
    for p in prefs:
        if n % p == 0:
            return p
    return n


def kernel(x_prompt, x_sample, cache_k, cache_v, state_pool, page_table, w_in, sb_bias, w_out, w_pool,
           pool_scale, ln1_g, ln1_b, w_query, sub_keys, u_table, v_table, ln2_g, ln2_b):
    depth = w_in.shape[0]
    b, s, d = x_prompt.shape
    db, t_new, _ = x_sample.shape
    n_pages = page_table.shape[1]
    past_len = n_pages * PAGE_SIZE
    alpha = (2.0 * depth) ** 0.25
    n_pool = cache_k.shape[1]

    hp, hs = x_prompt, x_sample
    outs = [[] for _ in range(6)]
    for l in range(depth):
        w_in_bf = w_in[l].astype(BF16)
        wkvt_bf = w_in[l][:, SB_WIDTH:3 * SB_WIDTH].T.astype(BF16)
        wout_bf = w_out[l].astype(BF16)
        wpool_bf = w_pool[l].astype(BF16)
        w_scores = _score_weights(sub_keys[l].reshape(2 * PEER_HEADS, PEER_N_KEYS, PEER_HALF), w_query[l])
        u_bf = u_table[l].astype(BF16)
        vt_bf = v_table[l].reshape(-1, PEER_CHUNK_ROWS * PEER_N_KEYS, d).transpose(0, 2, 1).astype(BF16)
        pscale = pool_scale[l].reshape(1, POOL_WIDTH)
        g1, b1 = ln1_g[l].reshape(1, d), ln1_b[l].reshape(1, d)
        g2, b2 = ln2_g[l].reshape(1, d), ln2_b[l].reshape(1, d)
        bias = sb_bias[l].astype(F32)
        peer = functools.partial(_peer, w_scores=w_scores, u_bf=u_bf, vt_bf=vt_bf,
                                 ln_g=g2, ln_b=b2, rows_per_chunk=PEER_CHUNK_ROWS, alpha=alpha)

        n = b * s
        q, kt, vt, u, kb, vb = _in_proj(hp.reshape(n, d), w_in_bf, wkvt_bf, _pick(s, (512, 256, 128)), seq=s)
        attn = _sb_prompt(bias, q.reshape(b, s, SB_WIDTH), kb.reshape(b, s, SB_WIDTH),
                          vb.reshape(b, s, SB_WIDTH), _pick(s, (256, 128)))
        u3 = u.reshape(b, s, POOL_WIDTH)
        x1 = _mix(hp, attn, u3, u3, wpool_bf, pscale, wout_bf, g1, b1,
                  bb=1, t=_pick(s, (512, 256, 128)), alpha=alpha, pos0=0, zero_first_prev=True)
        hp = peer(x1.reshape(n, d), tr=_pick(n, (256, 128)), tm=_pick(n, (512, 256, 128))).reshape(b, s, d)
        outs[0].append(kt.reshape(b, SB_HEADS, SB_HEAD_DIM, s).transpose(0, 3, 1, 2))
        outs[1].append(vt.reshape(b, SB_HEADS, SB_HEAD_DIM, s).transpose(0, 3, 1, 2))
        outs[2].append(u3[:, s - POOL_STATE_LEN:])

        ns = db * t_new
        qs, ks, vs, us, _, _ = _in_proj(hs.reshape(ns, d), w_in_bf, wkvt_bf, _pick(ns, (512, 256, 128)))
        attn_s = _sb_sample(page_table, bias, qs.reshape(db, t_new, SB_WIDTH),
                            ks.reshape(db, t_new, SB_WIDTH), vs.reshape(db, t_new, SB_WIDTH),
                            cache_k[l].transpose(0, 2, 3, 1), cache_v[l].transpose(0, 2, 3, 1),
                            _pick(n_pages, (8, 4, 2, 1)))
        us3 = us.reshape(db, t_new, POOL_WIDTH)
        state = state_pool[l].astype(F32)
        prev = jnp.pad(state, ((0, 0), (POOL_PREV_ROWS - POOL_STATE_LEN, 0), (0, 0)))
        x1s = _mix(hs, attn_s, us3, prev, wpool_bf, pscale, wout_bf, g1, b1,
                   bb=_pick(db, (16, 8, 4, 2, 1)), t=t_new, alpha=alpha, pos0=past_len, zero_first_prev=False)
        hs = peer(x1s.reshape(ns, d), tr=_pick(ns, (256, 128)), tm=_pick(ns, (512, 256, 128))).reshape(db, t_new, d)
        outs[3].append(ks.reshape(db, t_new, SB_HEADS, SB_HEAD_DIM))
        outs[4].append(vs.reshape(db, t_new, SB_HEADS, SB_HEAD_DIM))
        outs[5].append(jnp.concatenate([state, us3], axis=1)[:, -POOL_STATE_LEN:])

    return (hp, hs, jnp.stack(outs[0]), jnp.stack(outs[1]), jnp.stack(outs[2]),
            jnp.stack(outs[3]), jnp.stack(outs[4]), jnp.stack(outs[5]))
```

```python
import functools
import math

import jax
import jax.numpy as jnp
from jax import lax
from jax.experimental import pallas as pl
from jax.experimental.pallas import tpu as pltpu

F32 = jnp.float32
BF16 = jnp.bfloat16

SB_HEADS = 8
SB_HEAD_DIM = 64
SB_WIDTH = SB_HEADS * SB_HEAD_DIM
POOL_WINDOWS = (2, 4, 8, 16)
POOL_GROUP_DIM = 128
POOL_WIDTH = len(POOL_WINDOWS) * POOL_GROUP_DIM
POOL_STATE_LEN = max(POOL_WINDOWS) - 1
POOL_PREV_ROWS = 16
PAGE_SIZE = 128
PEER_HEADS = 8
PEER_N_KEYS = 128
PEER_TOPK = 16
PEER_HALF = 128
PEER_CHUNK_ROWS = 8
LN_EPS = 1e-5

LANES = 128
SUBLANES = 8
BF16_TILE_ROWS = 16
VMEM_LIMIT_BYTES = 56 * 1024 * 1024

NEG_BIG = -1e30
LOG2_E = 1.4426950408889634
_NT = (((1,), (1,)), ((), ()))


def _compiler_params(semantics):
    return pltpu.CompilerParams(dimension_semantics=semantics, vmem_limit_bytes=VMEM_LIMIT_BYTES)


def _proj_kernel(x_ref, w_ref, q_ref, k_ref, v_ref, u_ref, kb_ref, vb_ref, *, tokens_last):
    xb = x_ref[...].astype(BF16)
    w = SB_WIDTH
    q = jnp.dot(xb, w_ref[:, 0:w], preferred_element_type=F32)
    q_ref[...] = (q * -(SB_HEAD_DIM ** -0.5)).astype(BF16)
    k = jnp.dot(xb, w_ref[:, w:2 * w], preferred_element_type=F32)
    kb_ref[...] = k.astype(BF16)
    v = jnp.dot(xb, w_ref[:, 2 * w:3 * w], preferred_element_type=F32)
    vb_ref[...] = v.astype(BF16)
    u_ref[...] = jnp.dot(xb, w_ref[:, 3 * w:], preferred_element_type=F32)
    if tokens_last:
        k_ref[0] = k.T
        v_ref[0] = v.T
    else:
        k_ref[...] = k
        v_ref[...] = v


def _in_proj(x2d, w_in_bf, tm, seq=None):
    n, d = x2d.shape
    pw = w_in_bf.shape[1]
    row = lambda i: (i, 0)
    out_block = pl.BlockSpec((tm, SB_WIDTH), row)
    if seq is None:
        kv_block, kv_shape = out_block, (n, SB_WIDTH)
    else:
        assert seq % tm == 0 and n % seq == 0
        per_seq = seq // tm
        kv_block = pl.BlockSpec((1, SB_WIDTH, tm), lambda i: (i // per_seq, 0, i % per_seq))
        kv_shape = (n // seq, SB_WIDTH, seq)
    return pl.pallas_call(
        functools.partial(_proj_kernel, tokens_last=seq is not None),
        grid=(n // tm,),
        in_specs=[pl.BlockSpec((tm, d), row), pl.BlockSpec((d, pw), lambda i: (0, 0))],
        out_specs=[out_block, kv_block, kv_block, out_block, out_block, out_block],
        out_shape=[jax.ShapeDtypeStruct((n, SB_WIDTH), BF16),
                   jax.ShapeDtypeStruct(kv_shape, F32),
                   jax.ShapeDtypeStruct(kv_shape, F32),
                   jax.ShapeDtypeStruct((n, POOL_WIDTH), F32),
                   jax.ShapeDtypeStruct((n, SB_WIDTH), BF16),
                   jax.ShapeDtypeStruct((n, SB_WIDTH), BF16)],
        compiler_params=_compiler_params(("parallel",)),
        name="in_proj",
    )(x2d, w_in_bf)


def _suffix_sum_matrix(tk):
    r = lax.broadcasted_iota(jnp.int32, (tk, 2 * tk), 0)
    c = lax.broadcasted_iota(jnp.int32, (tk, 2 * tk), 1)
    return jnp.where((r > c) | (c >= tk), 1.0, 0.0).astype(BF16)


def _log_one_minus_beta(nz):
    neg_abs = lax.bitcast_convert_type(lax.bitcast_convert_type(nz, jnp.uint32) | jnp.uint32(1 << 31), F32)
    return jnp.minimum(nz, 0.0) - jnp.log(1.0 + jnp.exp2(neg_abs * LOG2_E))


def _sb_weights(nzs, carry, tmat, causal):
    rows, tk = nzs[0].shape
    log1m = [_log_one_minus_beta(nz) for nz in nzs]
    if causal is not None:
        log1m = [jnp.where(causal, l, 0.0) for l in log1m]
    sums = jnp.dot(jnp.concatenate(log1m, axis=0).astype(BF16), tmat, preferred_element_type=F32)
    ws = []
    for p, (nz, l) in enumerate(zip(nzs, log1m)):
        sp = sums[p * rows:(p + 1) * rows]
        w = jnp.exp((l - nz) + (sp[:, :tk] + carry))
        if causal is not None:
            w = jnp.where(causal, w, 0.0)
        ws.append(w.astype(BF16))
        carry = carry + sp[:, tk:]
    return ws, carry


def _sb_prompt_kernel(bias_ref, q_ref, k_ref, v_ref, o_ref,
                      qs_ref, zl_ref, l_ref, w_ref, carry_ref, acc_ref, *, tq):
    qi = pl.program_id(1)
    n_pairs = SB_HEADS // 2
    r = lax.broadcasted_iota(jnp.int32, (tq, tq), 0)
    c = lax.broadcasted_iota(jnp.int32, (tq, tq), 1)
    strict = c < r
    tmat = jnp.where(r > c, 1.0, 0.0).astype(BF16)
    lane = lax.broadcasted_iota(jnp.int32, (tq, LANES), 1)
    low = lane < SB_HEAD_DIM
    for hp in range(n_pairs):
        q = q_ref[0, :, hp * LANES:(hp + 1) * LANES]
        zero = jnp.zeros_like(q)
        qs_ref[hp, 0:tq, :] = jnp.where(low, q, zero)
        qs_ref[hp, tq:2 * tq, :] = jnp.where(low, zero, q)
    carry_ref[...] = jnp.zeros_like(carry_ref)
    acc_ref[...] = jnp.zeros_like(acc_ref)

    def block(j, causal):
        start = pl.multiple_of(j * tq, tq)
        for hp in range(n_pairs):
            kj = k_ref[0, pl.ds(start, tq), hp * LANES:(hp + 1) * LANES]
            z2 = lax.dot_general(qs_ref[hp], kj, (((1,), (1,)), ((), ())), preferred_element_type=F32)
            for h2 in range(2):
                h = 2 * hp + h2
                nz = z2[h2 * tq:(h2 + 1) * tq, :] - bias_ref[h]
                log1m = _log_one_minus_beta(nz)
                if causal:
                    log1m = jnp.where(strict, log1m, 0.0)
                zl_ref[h] = log1m - nz
                l_ref[h * tq:(h + 1) * tq, :] = log1m.astype(BF16)
        later = jnp.dot(l_ref[...], tmat, preferred_element_type=F32)
        for hp in range(n_pairs):
            for h2 in range(2):
                h = 2 * hp + h2
                rest = later[h * tq:(h + 1) * tq, :]
                first = l_ref[h * tq:(h + 1) * tq, 0:1].astype(F32)
                rowsum = jnp.broadcast_to(rest[:, 0:1] + first, (tq, LANES))
                carry = carry_ref[h]
                w = jnp.exp(zl_ref[h] + rest + jnp.concatenate([carry] * (tq // LANES), axis=1))
                if causal:
                    w = jnp.where(strict, w, 0.0)
                w_ref[hp, h2 * tq:(h2 + 1) * tq, :] = w.astype(BF16)
                carry_ref[h] = carry + rowsum
        for hp in range(n_pairs):
            vj = v_ref[0, pl.ds(start, tq), hp * LANES:(hp + 1) * LANES]
            acc_ref[hp] += jnp.dot(w_ref[hp], vj, preferred_element_type=F32)

    block(qi, True)

    def body(it, _):
        block(qi - 1 - it, False)
        return 0

    lax.fori_loop(0, qi, body, 0)
    for hp in range(n_pairs):
        o_ref[0, :, hp * LANES:(hp + 1) * LANES] = jnp.where(
            low, acc_ref[hp, 0:tq, :], acc_ref[hp, tq:2 * tq, :]).astype(o_ref.dtype)


def _sb_prompt(bias, q, kb, vb, tq):
    b, s, _ = q.shape
    assert tq % LANES == 0 and s % tq == 0
    qspec = pl.BlockSpec((1, tq, SB_WIDTH), lambda bi, qi: (bi, qi, 0))
    kvspec = pl.BlockSpec((1, s, SB_WIDTH), lambda bi, qi: (bi, 0, 0))
    n_pairs = SB_HEADS // 2
    return pl.pallas_call(
        functools.partial(_sb_prompt_kernel, tq=tq),
        grid=(b, s // tq),
        in_specs=[pl.BlockSpec(memory_space=pltpu.SMEM), qspec, kvspec, kvspec],
        out_specs=qspec,
        out_shape=jax.ShapeDtypeStruct((b, s, SB_WIDTH), BF16),
        scratch_shapes=[pltpu.VMEM((n_pairs, 2 * tq, LANES), BF16),
                        pltpu.VMEM((SB_HEADS, tq, tq), F32),
                        pltpu.VMEM((SB_HEADS * tq, tq), BF16),
                        pltpu.VMEM((n_pairs, 2 * tq, tq), BF16),
                        pltpu.VMEM((SB_HEADS, tq, LANES), F32),
                        pltpu.VMEM((n_pairs, 2 * tq, LANES), F32)],
        compiler_params=_compiler_params(("parallel", "arbitrary")),
        name="sb_prompt",
    )(bias, q, kb, vb)


def _sb_sample_kernel(pt_ref, bias_ref, q_ref, kn_ref, vn_ref, *rest, pages_per_step, t_new):
    del pt_ref
    kpages = rest[:pages_per_step]
    vpages = rest[pages_per_step:2 * pages_per_step]
    o_ref = rest[2 * pages_per_step]
    qbd_ref, biasrow_ref, carry_ref, acc_ref = rest[2 * pages_per_step + 1:]
    g = pl.program_id(1)
    rows = SB_HEADS * t_new
    tmat = _suffix_sum_matrix(PAGE_SIZE)

    def weights(zs, causal):
        ws, c = _sb_weights([z - biasrow_ref[...] for z in zs], carry_ref[...], tmat, causal)
        carry_ref[...] = c
        return ws

    @pl.when(g == 0)
    def _():
        r = lax.broadcasted_iota(jnp.int32, (rows, SB_WIDTH), 0)
        c = lax.broadcasted_iota(jnp.int32, (rows, SB_WIDTH), 1)
        qrep = jnp.concatenate([q_ref[0].astype(F32)] * SB_HEADS, axis=0)
        qbd_ref[...] = jnp.where(c // SB_HEAD_DIM == r // t_new, qrep, 0.0).astype(BF16)
        rr = lax.broadcasted_iota(jnp.int32, (rows, PAGE_SIZE), 0)
        brow = jnp.zeros((rows, PAGE_SIZE), F32)
        for h in range(SB_HEADS):
            brow = jnp.where(rr // t_new == h, bias_ref[h], brow)
        biasrow_ref[...] = brow
        carry_ref[...] = jnp.zeros_like(carry_ref)
        acc_ref[...] = jnp.zeros_like(acc_ref)
        pad = jnp.zeros((PAGE_SIZE - t_new, SB_WIDTH), F32)
        kn = jnp.concatenate([kn_ref[0], pad], axis=0).astype(BF16)
        vn = jnp.concatenate([vn_ref[0], pad], axis=0).astype(BF16)
        cc = lax.broadcasted_iota(jnp.int32, (rows, PAGE_SIZE), 1)
        w_new, = weights([lax.dot_general(qbd_ref[...], kn, _NT, preferred_element_type=F32)],
                         cc < rr % t_new)
        acc_ref[...] += jnp.dot(w_new, vn, preferred_element_type=F32)

    qbd = qbd_ref[...]
    ws = weights([jnp.dot(qbd, kp[0].reshape(SB_WIDTH, PAGE_SIZE).astype(BF16), preferred_element_type=F32)
                  for kp in kpages], None)
    vt = jnp.concatenate([vp[0].reshape(SB_WIDTH, PAGE_SIZE).astype(BF16) for vp in vpages], axis=1)
    acc_ref[...] += lax.dot_general(jnp.concatenate(ws, axis=1), vt, _NT, preferred_element_type=F32)

    @pl.when(g == pl.num_programs(1) - 1)
    def _():
        c = lax.broadcasted_iota(jnp.int32, (t_new, SB_WIDTH), 1)
        out = jnp.zeros((t_new, SB_WIDTH), F32)
        for h in range(SB_HEADS):
            out = jnp.where(c // SB_HEAD_DIM == h, acc_ref[h * t_new:(h + 1) * t_new, :], out)
        o_ref[0] = out.astype(o_ref.dtype)


def _sb_sample(page_table, bias, q, k_new, v_new, cache_k, cache_v, pages_per_step):
    b, t_new, _ = q.shape
    n_pages = page_table.shape[1]
    assert n_pages % pages_per_step == 0 and t_new % SUBLANES == 0
    steps = n_pages // pages_per_step
    rows = SB_HEADS * t_new

    def page_spec(p):
        return pl.BlockSpec((1, SB_HEADS, SB_HEAD_DIM, PAGE_SIZE),
                            lambda bi, g, pt: (pt[bi, n_pages - 1 - (g * pages_per_step + p)], 0, 0, 0))

    tok = lambda dt: pl.BlockSpec((1, t_new, SB_WIDTH), lambda bi, g, pt: (bi, 0, 0))
    grid_spec = pltpu.PrefetchScalarGridSpec(
        num_scalar_prefetch=1,
        grid=(b, steps),
        in_specs=[pl.BlockSpec(memory_space=pltpu.SMEM), tok(BF16), tok(F32), tok(F32)]
        + [page_spec(p) for p in range(pages_per_step)] * 2,
        out_specs=pl.BlockSpec((1, t_new, SB_WIDTH), lambda bi, g, pt: (bi, 0, 0)),
        scratch_shapes=[pltpu.VMEM((rows, SB_WIDTH), BF16),
                        pltpu.VMEM((rows, PAGE_SIZE), F32),
                        pltpu.VMEM((rows, PAGE_SIZE), F32),
                        pltpu.VMEM((rows, SB_WIDTH), F32)],
    )
    return pl.pallas_call(
        functools.partial(_sb_sample_kernel, pages_per_step=pages_per_step, t_new=t_new),
        grid_spec=grid_spec,
        out_shape=jax.ShapeDtypeStruct((b, t_new, SB_WIDTH), F32),
        compiler_params=_compiler_params(("parallel", "arbitrary")),
        name="sb_sample",
    )(page_table, bias, q, k_new, v_new,
      *([cache_k] * pages_per_step), *([cache_v] * pages_per_step))


def _layer_norm(x, g, b):
    mu = jnp.mean(x, axis=-1, keepdims=True)
    xc = x - mu
    var = jnp.mean(xc * xc, axis=-1, keepdims=True)
    return xc * lax.rsqrt(var + LN_EPS) * g + b


def _mix_kernel(x_ref, attn_ref, u_ref, prev_ref, wpool_ref, pscale_ref, wout_ref, g_ref, b_ref, o_ref,
                *, alpha, pos0, zero_first_prev):
    bb, t, _ = u_ref.shape
    i = pl.program_id(1)
    prev = prev_ref[...]
    if zero_first_prev:
        prev = jnp.where(i == 0, 0.0, prev)
    ext = jnp.concatenate([prev, u_ref[...]], axis=1)
    pos = pos0 + i * t + lax.broadcasted_iota(jnp.int32, (1, t, 1), 1)
    mixed = []
    for gi, win in enumerate(POOL_WINDOWS):
        sl = slice(gi * POOL_GROUP_DIM, (gi + 1) * POOL_GROUP_DIM)
        acc = ext[:, :, sl]
        span = 1
        while span < win:
            acc = acc[:, span:, :] + acc[:, :acc.shape[1] - span, :]
            span *= 2
        tok = ext[:, POOL_PREV_ROWS:, sl]
        inv = 1.0 / jnp.minimum(win, pos + 1).astype(F32)
        pooled = acc[:, acc.shape[1] - t:, :] * inv - tok
        pooled = pooled.reshape(bb * t, POOL_GROUP_DIM).astype(BF16)
        mixed.append(jnp.dot(pooled, wpool_ref[gi], preferred_element_type=F32))
    pool_out = (jnp.concatenate(mixed, axis=-1) * pscale_ref[...]).astype(BF16)
    attn = attn_ref[...].reshape(bb * t, SB_WIDTH).astype(BF16)
    mix = (jnp.dot(attn, wout_ref[0:SB_WIDTH, :], preferred_element_type=F32)
           + jnp.dot(pool_out, wout_ref[SB_WIDTH:, :], preferred_element_type=F32))
    x = x_ref[...].reshape(bb * t, x_ref.shape[2])
    o_ref[...] = _layer_norm(alpha * x + mix, g_ref[...], b_ref[...]).reshape(o_ref.shape)


def _mix(x, attn, u, prev, wpool_bf, pscale, wout_bf, ln_g, ln_b, *, bb, t, alpha, pos0, zero_first_prev):
    b, s, d = x.shape
    assert b % bb == 0 and s % t == 0 and t % SUBLANES == 0
    if zero_first_prev:
        assert t % POOL_PREV_ROWS == 0
        prev_map = lambda bi, i: (bi, jnp.maximum(i * (t // POOL_PREV_ROWS) - 1, 0), 0)
    else:
        assert s == t
        prev_map = lambda bi, i: (bi, 0, 0)
    tile = lambda w: pl.BlockSpec((bb, t, w), lambda bi, i: (bi, i, 0))
    const2 = lambda shape: pl.BlockSpec(shape, lambda bi, i: (0, 0))
    return pl.pallas_call(
        functools.partial(_mix_kernel, alpha=alpha, pos0=pos0, zero_first_prev=zero_first_prev),
        grid=(b // bb, s // t),
        in_specs=[tile(d), tile(SB_WIDTH), tile(POOL_WIDTH),
                  pl.BlockSpec((bb, POOL_PREV_ROWS, POOL_WIDTH), prev_map),
                  pl.BlockSpec(wpool_bf.shape, lambda bi, i: (0, 0, 0)),
                  const2((1, POOL_WIDTH)), const2(wout_bf.shape), const2((1, d)), const2((1, d))],
        out_specs=tile(d),
        out_shape=jax.ShapeDtypeStruct((b, s, d), F32),
        compiler_params=_compiler_params(("parallel", "arbitrary")),
        name="mix_ln1",
    )(x, attn, u, prev, wpool_bf, pscale, wout_bf, ln_g, ln_b)


def _sorting_network(n):
    pairs = []

    def merge(lo, m, r):
        step = 2 * r
        if step < m:
            merge(lo, m, step)
            merge(lo + r, m, step)
            pairs.extend((i, i + r) for i in range(lo + r, lo + m - r, step))
        else:
            pairs.append((lo, lo + r))

    def sort(lo, m):
        if m > 1:
            sort(lo, m // 2)
            sort(lo + m // 2, m // 2)
            merge(lo, m, 1)

    sort(0, n)
    return pairs


def _top_values(s, count):
    n = s.shape[0] // SUBLANES
    v = [s[SUBLANES * k:SUBLANES * (k + 1), :] for k in range(n)]
    for i, j in _sorting_network(n):
        v[i], v[j] = jnp.maximum(v[i], v[j]), jnp.minimum(v[i], v[j])
    ridx = lax.broadcasted_iota(jnp.int32, (count, s.shape[1]), 0)
    top = jnp.full((count, s.shape[1]), NEG_BIG, F32)
    for r in range(count):
        m = jnp.max(v[0], axis=0, keepdims=True)
        top = jnp.where(ridx == r, m, top)
        hit = v[0] == m
        need = count - r - 1
        for k in range(min(need, n - 1)):
            v[k] = jnp.where(hit, v[k + 1], v[k])
        if need >= n:
            v[n - 1] = jnp.where(hit, NEG_BIG, v[n - 1])
    return top


_PACKED_PAIRS = [(k, kp) for k in range(2, PEER_TOPK) for kp in range(PEER_TOPK // (k + 1))]


def _candidate_sums(a, b):
    t = a.shape[1]
    row = lax.broadcasted_iota(jnp.int32, (SUBLANES, t), 0)
    groups = [a[0:1, :] + b, a[1:2, :] + b[0:SUBLANES, :]]
    for g in range(0, len(_PACKED_PAIRS), SUBLANES):
        a_pat = jnp.full((SUBLANES, t), NEG_BIG, F32)
        b_pat = jnp.zeros((SUBLANES, t), F32)
        for r, (k, kp) in enumerate(_PACKED_PAIRS[g:g + SUBLANES]):
            a_pat = jnp.where(row == r, a[k:k + 1, :], a_pat)
            b_pat = jnp.where(row == r, b[kp:kp + 1, :], b_pat)
        groups.append(a_pat + b_pat)
    n_groups = sum(g.shape[0] for g in groups) // SUBLANES
    pad = (1 << (n_groups - 1).bit_length()) - n_groups
    groups += [jnp.full((SUBLANES, t), NEG_BIG, F32)] * pad
    return jnp.concatenate(groups, axis=0)


def _route_tile(s1, s2):
    k_top = PEER_TOPK
    a = _top_values(s1, k_top)
    b = _top_values(s2, k_top)
    cv = _top_values(_candidate_sums(a, b), k_top)
    least = cv[k_top - 1:k_top, :]
    z = jnp.sum(jnp.exp(cv - cv[0:1, :]), axis=0, keepdims=True)
    rank2 = jnp.full_like(s2, float(k_top))
    for k in reversed(range(k_top)):
        rank2 = jnp.where(s2 >= b[k:k + 1, :], float(k), rank2)
    cnt = jnp.zeros_like(s1)
    for k in range(SUBLANES):
        cnt = jnp.where(s1 + b[k:k + 1, :] >= least, float(k + 1), cnt)
    tail = jnp.sum(jnp.where(a[0:1, :] + b[SUBLANES:, :] >= least, 1.0, 0.0), axis=0, keepdims=True)
    cnt = cnt + jnp.where(s1 == a[0:1, :], tail, 0.0)
    return rank2, jnp.exp(s2 - b[0:1, :]), cnt, jnp.exp(s1 - a[0:1, :]) / z


def _score_weights_kernel(keys_ref, wq_ref, o_ref):
    o_ref[...] = lax.dot_general(keys_ref[0], wq_ref[...], _NT, precision=lax.Precision.HIGHEST,
                                 preferred_element_type=F32).astype(o_ref.dtype)


def _score_weights(keys, w_query):
    n_hp, n_keys, half = keys.shape
    d = w_query.shape[0]
    return pl.pallas_call(
        _score_weights_kernel,
        grid=(n_hp,),
        in_specs=[pl.BlockSpec((1, n_keys, half), lambda i: (i, 0, 0)),
                  pl.BlockSpec((d, half), lambda i: (0, i))],
        out_specs=pl.BlockSpec((n_keys, d), lambda i: (i, 0)),
        out_shape=jax.ShapeDtypeStruct((n_hp * n_keys, d), BF16),
        compiler_params=_compiler_params(("parallel",)),
        name="peer_score_weights",
    )(keys, w_query)


def _peer_route_kernel(x0_ref, xn_ref, w_ref, rk_ref, eb_ref, cnt_ref, ea_ref, s_ref):
    i = pl.program_id(0)

    def scores(x_ref, slot):
        s_ref[slot] = lax.dot_general(w_ref[...], x_ref[...].astype(BF16), _NT, preferred_element_type=F32)

    @pl.when(i == 0)
    def _():
        scores(x0_ref, 0)

    for cur in range(2):
        @pl.when(i % 2 == cur)
        def _():
            scores(xn_ref, 1 - cur)
            for h in range(PEER_HEADS):
                r1 = 2 * h * PEER_N_KEYS
                r2 = r1 + PEER_N_KEYS
                for lt in range(rk_ref.shape[2] // LANES):
                    sl = slice(lt * LANES, (lt + 1) * LANES)
                    rank2, eb, cnt, ea = _route_tile(s_ref[cur, r1:r2, sl], s_ref[cur, r2:r2 + PEER_N_KEYS, sl])
                    rk_ref[h, :, sl] = rank2.astype(BF16)
                    eb_ref[h, :, sl] = eb.astype(BF16)
                    cnt_ref[h, :, sl] = cnt
                    ea_ref[h, :, sl] = ea


def _peer_route(x1, w_scores, tr):
    n, d = x1.shape
    assert n % tr == 0 and tr % LANES == 0
    steps = n // tr
    out_block = pl.BlockSpec((PEER_HEADS, PEER_N_KEYS, tr), lambda i: (0, 0, i))
    shape = (PEER_HEADS, PEER_N_KEYS, n)
    return pl.pallas_call(
        _peer_route_kernel,
        grid=(steps,),
        in_specs=[pl.BlockSpec((tr, d), lambda i: (0, 0)),
                  pl.BlockSpec((tr, d), lambda i: (jnp.minimum(i + 1, steps - 1), 0)),
                  pl.BlockSpec(w_scores.shape, lambda i: (0, 0))],
        out_specs=[out_block] * 4,
        out_shape=[jax.ShapeDtypeStruct(shape, BF16), jax.ShapeDtypeStruct(shape, BF16),
                   jax.ShapeDtypeStruct(shape, F32), jax.ShapeDtypeStruct(shape, F32)],
        scratch_shapes=[pltpu.VMEM((2, w_scores.shape[0], tr), F32)],
        compiler_params=_compiler_params(("arbitrary",)),
        name="peer_route",
    )(x1, x1, w_scores)


def _peer_chunk(act_ref, i0, vt_ref, rk_ref, eb_ref, cnt_ref, ea_ref, rows_per_chunk):
    tm = act_ref.shape[1]
    tile = (BF16_TILE_ROWS, tm)
    tiles = (PEER_N_KEYS // BF16_TILE_ROWS, BF16_TILE_ROWS, tm)
    total = None
    for pair in range(rows_per_chunk // 2):
        rows = (2 * pair, 2 * pair + 1)
        gates = [None, None]
        for h in range(PEER_HEADS):
            rk = rk_ref[h].reshape(tiles)
            eb = eb_ref[h].reshape(tiles)
            for slot, ii in enumerate(rows):
                cnt_i = jnp.broadcast_to(cnt_ref[h, pl.ds(i0 + ii, 1), :], tile).astype(BF16)
                ea_i = jnp.broadcast_to(ea_ref[h, pl.ds(i0 + ii, 1), :], tile).astype(BF16)
                term = jnp.where(rk < cnt_i[None], ea_i[None] * eb, jnp.zeros_like(eb))
                gates[slot] = term if gates[slot] is None else gates[slot] + term
        coef = []
        for gate, ii in zip(gates, rows):
            a = act_ref[ii * PEER_N_KEYS:(ii + 1) * PEER_N_KEYS, :]
            gelu = 0.5 * a * (1.0 + lax.erf(a * (2.0 ** -0.5)))
            coef.append(gate.reshape(PEER_N_KEYS, tm) * gelu.astype(BF16))
        lo = 2 * pair * PEER_N_KEYS
        part = jnp.dot(vt_ref[0, :, lo:lo + 2 * PEER_N_KEYS], jnp.concatenate(coef, axis=0),
                       preferred_element_type=F32)
        total = part if total is None else total + part
    return total


def _peer_kernel(x_ref, u0_ref, un_ref, vt_ref, rk_ref, eb_ref, cnt_ref, ea_ref, g_ref, b_ref, o_ref,
                 xb_ref, act_ref, acc_ref, *, alpha, rows_per_chunk):
    c = pl.program_id(1)
    route = (rk_ref, eb_ref, cnt_ref, ea_ref)

    @pl.when(c == 0)
    def _():
        xb_ref[...] = x_ref[...].T.astype(BF16)
        act_ref[0] = jnp.dot(u0_ref[...], xb_ref[...], preferred_element_type=F32)
        acc_ref[...] = jnp.zeros_like(acc_ref)

    i0 = pl.multiple_of(c * rows_per_chunk, rows_per_chunk)
    for cur in range(2):
        @pl.when(c % 2 == cur)
        def _():
            act_ref[1 - cur] = jnp.dot(un_ref[...], xb_ref[...], preferred_element_type=F32)
            acc_ref[...] += _peer_chunk(act_ref.at[cur], i0, vt_ref, *route, rows_per_chunk)

    @pl.when(c == pl.num_programs(1) - 1)
    def _():
        peer = acc_ref[...].T
        o_ref[...] = _layer_norm(alpha * x_ref[...] + peer, g_ref[...], b_ref[...])


def _peer(x1, w_scores, u_bf, vt_bf, ln_g, ln_b, *, tr, tm, rows_per_chunk, alpha):
    n, d = x1.shape
    n_exp = u_bf.shape[0]
    chunk = rows_per_chunk * PEER_N_KEYS
    assert n % tm == 0 and n_exp % (2 * chunk) == 0 and rows_per_chunk % SUBLANES == 0
    n_chunks = n_exp // chunk
    routed = _peer_route(x1, w_scores, tr)
    const = lambda shape: pl.BlockSpec(shape, lambda i, c: (0,) * len(shape))
    route_block = pl.BlockSpec((PEER_HEADS, PEER_N_KEYS, tm), lambda i, c: (0, 0, i))
    return pl.pallas_call(
        functools.partial(_peer_kernel, alpha=alpha, rows_per_chunk=rows_per_chunk),
        grid=(n // tm, n_chunks),
        in_specs=[pl.BlockSpec((tm, d), lambda i, c: (i, 0)),
                  pl.BlockSpec((chunk, d), lambda i, c: (0, 0)),
                  pl.BlockSpec((chunk, d), lambda i, c: (jnp.minimum(c + 1, n_chunks - 1), 0)),
                  pl.BlockSpec((1, d, chunk), lambda i, c: (c, 0, 0)),
                  route_block, route_block, route_block, route_block,
                  const((1, d)), const((1, d))],
        out_specs=pl.BlockSpec((tm, d), lambda i, c: (i, 0)),
        out_shape=jax.ShapeDtypeStruct((n, d), F32),
        scratch_shapes=[pltpu.VMEM((d, tm), BF16), pltpu.VMEM((2, chunk, tm), F32), pltpu.VMEM((d, tm), F32)],
        compiler_params=_compiler_params(("parallel", "arbitrary")),
        name="peer_ln2",
    )(x1, u_bf, u_bf, vt_bf, *routed, ln_g, ln_b)


def _pick(n, prefs):
    for p in prefs:
        if n % p == 0:
            return p
    return n


def kernel(x_prompt, x_sample, cache_k, cache_v, state_pool, page_table, w_in, sb_bias, w_out, w_pool,
           pool_scale, ln1_g, ln1_b, w_query, sub_keys, u_table, v_table, ln2_g, ln2_b):
    depth = w_in.shape[0]
    b, s, d = x_prompt.shape
    db, t_new, _ = x_sample.shape
    n_pages = page_table.shape[1]
    past_len = n_pages * PAGE_SIZE
    alpha = (2.0 * depth) ** 0.25
    n_pool = cache_k.shape[1]

    hp, hs = x_prompt, x_sample
    outs = [[] for _ in range(6)]
    for l in range(depth):
        w_in_bf = w_in[l].astype(BF16)
        wout_bf = w_out[l].astype(BF16)
        wpool_bf = w_pool[l].astype(BF16)
        w_scores = _score_weights(sub_keys[l].reshape(2 * PEER_HEADS, PEER_N_KEYS, PEER_HALF), w_query[l])
        u_bf = u_table[l].astype(BF16)
        vt_bf = v_table[l].reshape(-1, PEER_CHUNK_ROWS * PEER_N_KEYS, d).transpose(0, 2, 1).astype(BF16)
        pscale = pool_scale[l].reshape(1, POOL_WIDTH)
        g1, b1 = ln1_g[l].reshape(1, d), ln1_b[l].reshape(1, d)
        g2, b2 = ln2_g[l].reshape(1, d), ln2_b[l].reshape(1, d)
        bias = sb_bias[l].astype(F32)
        peer = functools.partial(_peer, w_scores=w_scores, u_bf=u_bf, vt_bf=vt_bf,
                                 ln_g=g2, ln_b=b2, rows_per_chunk=PEER_CHUNK_ROWS, alpha=alpha)

        n = b * s
        q, kt, vt, u, kb, vb = _in_proj(hp.reshape(n, d), w_in_bf, _pick(s, (512, 256, 128)), seq=s)
        attn = _sb_prompt(bias, q.reshape(b, s, SB_WIDTH), kb.reshape(b, s, SB_WIDTH),
                          vb.reshape(b, s, SB_WIDTH), _pick(s, (256, 128)))
        u3 = u.reshape(b, s, POOL_WIDTH)
        x1 = _mix(hp, attn, u3, u3, wpool_bf, pscale, wout_bf, g1, b1,
                  bb=1, t=_pick(s, (512, 256, 128)), alpha=alpha, pos0=0, zero_first_prev=True)
        hp = peer(x1.reshape(n, d), tr=_pick(n, (256, 128)), tm=_pick(n, (512, 256, 128))).reshape(b, s, d)
        outs[0].append(kt.reshape(b, SB_HEADS, SB_HEAD_DIM, s).transpose(0, 3, 1, 2))
        outs[1].append(vt.reshape(b, SB_HEADS, SB_HEAD_DIM, s).transpose(0, 3, 1, 2))
        outs[2].append(u3[:, s - POOL_STATE_LEN:])

        ns = db * t_new
        qs, ks, vs, us, _, _ = _in_proj(hs.reshape(ns, d), w_in_bf, _pick(ns, (512, 256, 128)))
        attn_s = _sb_sample(page_table, bias, qs.reshape(db, t_new, SB_WIDTH),
                            ks.reshape(db, t_new, SB_WIDTH), vs.reshape(db, t_new, SB_WIDTH),
                            cache_k[l].transpose(0, 2, 3, 1), cache_v[l].transpose(0, 2, 3, 1),
                            _pick(n_pages, (16, 8, 4, 2, 1)))
        us3 = us.reshape(db, t_new, POOL_WIDTH)
        state = state_pool[l].astype(F32)
        prev = jnp.pad(state, ((0, 0), (POOL_PREV_ROWS - POOL_STATE_LEN, 0), (0, 0)))
        x1s = _mix(hs, attn_s, us3, prev, wpool_bf, pscale, wout_bf, g1, b1,
                   bb=_pick(db, (16, 8, 4, 2, 1)), t=t_new, alpha=alpha, pos0=past_len, zero_first_prev=False)
        hs = peer(x1s.reshape(ns, d), tr=_pick(ns, (256, 128)), tm=_pick(ns, (512, 256, 128))).reshape(db, t_new, d)
        outs[3].append(ks.reshape(db, t_new, SB_HEADS, SB_HEAD_DIM))
        outs[4].append(vs.reshape(db, t_new, SB_HEADS, SB_HEAD_DIM))
        outs[5].append(jnp.concatenate([state, us3], axis=1)[:, -POOL_STATE_LEN:])

    return (hp, hs, jnp.stack(outs[0]), jnp.stack(outs[1]), jnp.stack(outs[2]),
            jnp.stack(outs[3]), jnp.stack(outs[4]), jnp.stack(outs[5]))
```

```python
import functools
import math

import jax
import jax.numpy as jnp
from jax import lax
from jax.experimental import pallas as pl
from jax.experimental.pallas import tpu as pltpu

F32 = jnp.float32
BF16 = jnp.bfloat16

SB_HEADS = 8
SB_HEAD_DIM = 64
SB_WIDTH = SB_HEADS * SB_HEAD_DIM
POOL_WINDOWS = (2, 4, 8, 16)
POOL_GROUP_DIM = 128
POOL_WIDTH = len(POOL_WINDOWS) * POOL_GROUP_DIM
POOL_STATE_LEN = max(POOL_WINDOWS) - 1
POOL_PREV_ROWS = 16
PAGE_SIZE = 128
PEER_HEADS = 8
PEER_N_KEYS = 128
PEER_TOPK = 16
PEER_HALF = 128
PEER_CHUNK_ROWS = 8
LN_EPS = 1e-5

LANES = 128
SUBLANES = 8
BF16_TILE_ROWS = 16
VMEM_LIMIT_BYTES = 56 * 1024 * 1024

NEG_BIG = -1e30
LOG2_E = 1.4426950408889634
_NT = (((1,), (1,)), ((), ()))


def _compiler_params(semantics):
    return pltpu.CompilerParams(dimension_semantics=semantics, vmem_limit_bytes=VMEM_LIMIT_BYTES)


def _proj_kernel(x_ref, w_ref, q_ref, k_ref, v_ref, u_ref, kb_ref, vb_ref, *, tokens_last):
    xb = x_ref[...].astype(BF16)
    w = SB_WIDTH
    q = jnp.dot(xb, w_ref[:, 0:w], preferred_element_type=F32)
    q_ref[...] = (q * -(SB_HEAD_DIM ** -0.5)).astype(BF16)
    k = jnp.dot(xb, w_ref[:, w:2 * w], preferred_element_type=F32)
    kb_ref[...] = k.astype(BF16)
    v = jnp.dot(xb, w_ref[:, 2 * w:3 * w], preferred_element_type=F32)
    vb_ref[...] = v.astype(BF16)
    u_ref[...] = jnp.dot(xb, w_ref[:, 3 * w:], preferred_element_type=F32)
    if tokens_last:
        k_ref[0] = k.T
        v_ref[0] = v.T
    else:
        k_ref[...] = k
        v_ref[...] = v


def _in_proj(x2d, w_in_bf, tm, seq=None):
    n, d = x2d.shape
    pw = w_in_bf.shape[1]
    row = lambda i: (i, 0)
    out_block = pl.BlockSpec((tm, SB_WIDTH), row)
    if seq is None:
        kv_block, kv_shape = out_block, (n, SB_WIDTH)
    else:
        assert seq % tm == 0 and n % seq == 0
        per_seq = seq // tm
        kv_block = pl.BlockSpec((1, SB_WIDTH, tm), lambda i: (i // per_seq, 0, i % per_seq))
        kv_shape = (n // seq, SB_WIDTH, seq)
    return pl.pallas_call(
        functools.partial(_proj_kernel, tokens_last=seq is not None),
        grid=(n // tm,),
        in_specs=[pl.BlockSpec((tm, d), row), pl.BlockSpec((d, pw), lambda i: (0, 0))],
        out_specs=[out_block, kv_block, kv_block, out_block, out_block, out_block],
        out_shape=[jax.ShapeDtypeStruct((n, SB_WIDTH), BF16),
                   jax.ShapeDtypeStruct(kv_shape, F32),
                   jax.ShapeDtypeStruct(kv_shape, F32),
                   jax.ShapeDtypeStruct((n, POOL_WIDTH), F32),
                   jax.ShapeDtypeStruct((n, SB_WIDTH), BF16),
                   jax.ShapeDtypeStruct((n, SB_WIDTH), BF16)],
        compiler_params=_compiler_params(("parallel",)),
        name="in_proj",
    )(x2d, w_in_bf)


def _suffix_sum_matrix(tk):
    r = lax.broadcasted_iota(jnp.int32, (tk, 2 * tk), 0)
    c = lax.broadcasted_iota(jnp.int32, (tk, 2 * tk), 1)
    return jnp.where((r > c) | (c >= tk), 1.0, 0.0).astype(BF16)


def _log_one_minus_beta(nz):
    neg_abs = lax.bitcast_convert_type(lax.bitcast_convert_type(nz, jnp.uint32) | jnp.uint32(1 << 31), F32)
    return jnp.minimum(nz, 0.0) - jnp.log(1.0 + jnp.exp2(neg_abs * LOG2_E))


def _sb_weights(nzs, carry, tmat, causal):
    rows, tk = nzs[0].shape
    log1m = [_log_one_minus_beta(nz) for nz in nzs]
    if causal is not None:
        log1m = [jnp.where(causal, l, 0.0) for l in log1m]
    sums = jnp.dot(jnp.concatenate(log1m, axis=0).astype(BF16), tmat, preferred_element_type=F32)
    ws = []
    for p, (nz, l) in enumerate(zip(nzs, log1m)):
        sp = sums[p * rows:(p + 1) * rows]
        w = jnp.exp((l - nz) + (sp[:, :tk] + carry))
        if causal is not None:
            w = jnp.where(causal, w, 0.0)
        ws.append(w.astype(BF16))
        carry = carry + sp[:, tk:]
    return ws, carry


def _sb_prompt_kernel(bias_ref, q_ref, k_ref, v_ref, o_ref,
                      qs_ref, zl_ref, l_ref, w_ref, carry_ref, acc_ref, *, tq):
    qi = pl.program_id(1)
    n_pairs = SB_HEADS // 2
    r = lax.broadcasted_iota(jnp.int32, (tq, tq), 0)
    c = lax.broadcasted_iota(jnp.int32, (tq, tq), 1)
    strict = c < r
    tmat = jnp.where(r > c, 1.0, 0.0).astype(BF16)
    lane = lax.broadcasted_iota(jnp.int32, (tq, LANES), 1)
    low = lane < SB_HEAD_DIM
    for hp in range(n_pairs):
        q = q_ref[0, :, hp * LANES:(hp + 1) * LANES]
        zero = jnp.zeros_like(q)
        qs_ref[hp, 0:tq, :] = jnp.where(low, q, zero)
        qs_ref[hp, tq:2 * tq, :] = jnp.where(low, zero, q)
    carry_ref[...] = jnp.zeros_like(carry_ref)
    acc_ref[...] = jnp.zeros_like(acc_ref)

    def block(j, causal):
        start = pl.multiple_of(j * tq, tq)
        for hp in range(n_pairs):
            kj = k_ref[0, pl.ds(start, tq), hp * LANES:(hp + 1) * LANES]
            z2 = lax.dot_general(qs_ref[hp], kj, (((1,), (1,)), ((), ())), preferred_element_type=F32)
            for h2 in range(2):
                h = 2 * hp + h2
                nz = z2[h2 * tq:(h2 + 1) * tq, :] - bias_ref[h]
                log1m = _log_one_minus_beta(nz)
                if causal:
                    log1m = jnp.where(strict, log1m, 0.0)
                zl_ref[h] = log1m - nz
                l_ref[h * tq:(h + 1) * tq, :] = log1m.astype(BF16)
        later = jnp.dot(l_ref[...], tmat, preferred_element_type=F32)
        for hp in range(n_pairs):
            for h2 in range(2):
                h = 2 * hp + h2
                rest = later[h * tq:(h + 1) * tq, :]
                first = l_ref[h * tq:(h + 1) * tq, 0:1].astype(F32)
                rowsum = jnp.broadcast_to(rest[:, 0:1] + first, (tq, LANES))
                carry = carry_ref[h]
                w = jnp.exp(zl_ref[h] + rest + jnp.concatenate([carry] * (tq // LANES), axis=1))
                if causal:
                    w = jnp.where(strict, w, 0.0)
                w_ref[hp, h2 * tq:(h2 + 1) * tq, :] = w.astype(BF16)
                carry_ref[h] = carry + rowsum
        for hp in range(n_pairs):
            vj = v_ref[0, pl.ds(start, tq), hp * LANES:(hp + 1) * LANES]
            acc_ref[hp] += jnp.dot(w_ref[hp], vj, preferred_element_type=F32)

    block(qi, True)

    def body(it, _):
        block(qi - 1 - it, False)
        return 0

    lax.fori_loop(0, qi, body, 0)
    for hp in range(n_pairs):
        o_ref[0, :, hp * LANES:(hp + 1) * LANES] = jnp.where(
            low, acc_ref[hp, 0:tq, :], acc_ref[hp, tq:2 * tq, :]).astype(o_ref.dtype)


def _sb_prompt(bias, q, kb, vb, tq):
    b, s, _ = q.shape
    assert tq % LANES == 0 and s % tq == 0
    qspec = pl.BlockSpec((1, tq, SB_WIDTH), lambda bi, qi: (bi, qi, 0))
    kvspec = pl.BlockSpec((1, s, SB_WIDTH), lambda bi, qi: (bi, 0, 0))
    n_pairs = SB_HEADS // 2
    return pl.pallas_call(
        functools.partial(_sb_prompt_kernel, tq=tq),
        grid=(b, s // tq),
        in_specs=[pl.BlockSpec(memory_space=pltpu.SMEM), qspec, kvspec, kvspec],
        out_specs=qspec,
        out_shape=jax.ShapeDtypeStruct((b, s, SB_WIDTH), BF16),
        scratch_shapes=[pltpu.VMEM((n_pairs, 2 * tq, LANES), BF16),
                        pltpu.VMEM((SB_HEADS, tq, tq), F32),
                        pltpu.VMEM((SB_HEADS * tq, tq), BF16),
                        pltpu.VMEM((n_pairs, 2 * tq, tq), BF16),
                        pltpu.VMEM((SB_HEADS, tq, LANES), F32),
                        pltpu.VMEM((n_pairs, 2 * tq, LANES), F32)],
        compiler_params=_compiler_params(("parallel", "arbitrary")),
        name="sb_prompt",
    )(bias, q, kb, vb)


def _sb_sample_kernel(pt_ref, bias_ref, q_ref, kn_ref, vn_ref, *rest, pages_per_step, t_new):
    del pt_ref
    kpages = rest[:pages_per_step]
    vpages = rest[pages_per_step:2 * pages_per_step]
    o_ref = rest[2 * pages_per_step]
    qbd_ref, biasrow_ref, carry_ref, acc_ref = rest[2 * pages_per_step + 1:]
    g = pl.program_id(1)
    rows = SB_HEADS * t_new
    tmat = _suffix_sum_matrix(PAGE_SIZE)

    def weights(zs, causal):
        ws, c = _sb_weights([z - biasrow_ref[...] for z in zs], carry_ref[...], tmat, causal)
        carry_ref[...] = c
        return ws

    @pl.when(g == 0)
    def _():
        r = lax.broadcasted_iota(jnp.int32, (rows, SB_WIDTH), 0)
        c = lax.broadcasted_iota(jnp.int32, (rows, SB_WIDTH), 1)
        qrep = jnp.concatenate([q_ref[0].astype(F32)] * SB_HEADS, axis=0)
        qbd_ref[...] = jnp.where(c // SB_HEAD_DIM == r // t_new, qrep, 0.0).astype(BF16)
        rr = lax.broadcasted_iota(jnp.int32, (rows, PAGE_SIZE), 0)
        brow = jnp.zeros((rows, PAGE_SIZE), F32)
        for h in range(SB_HEADS):
            brow = jnp.where(rr // t_new == h, bias_ref[h], brow)
        biasrow_ref[...] = brow
        carry_ref[...] = jnp.zeros_like(carry_ref)
        acc_ref[...] = jnp.zeros_like(acc_ref)
        pad = jnp.zeros((PAGE_SIZE - t_new, SB_WIDTH), F32)
        kn = jnp.concatenate([kn_ref[0], pad], axis=0).astype(BF16)
        vn = jnp.concatenate([vn_ref[0], pad], axis=0).astype(BF16)
        cc = lax.broadcasted_iota(jnp.int32, (rows, PAGE_SIZE), 1)
        w_new, = weights([lax.dot_general(qbd_ref[...], kn, _NT, preferred_element_type=F32)],
                         cc < rr % t_new)
        acc_ref[...] += jnp.dot(w_new, vn, preferred_element_type=F32)

    qbd = qbd_ref[...]
    ws = weights([jnp.dot(qbd, kp[0].reshape(SB_WIDTH, PAGE_SIZE).astype(BF16), preferred_element_type=F32)
                  for kp in kpages], None)
    vt = jnp.concatenate([vp[0].reshape(SB_WIDTH, PAGE_SIZE).astype(BF16) for vp in vpages], axis=1)
    acc_ref[...] += lax.dot_general(jnp.concatenate(ws, axis=1), vt, _NT, preferred_element_type=F32)

    @pl.when(g == pl.num_programs(1) - 1)
    def _():
        c = lax.broadcasted_iota(jnp.int32, (t_new, SB_WIDTH), 1)
        out = jnp.zeros((t_new, SB_WIDTH), F32)
        for h in range(SB_HEADS):
            out = jnp.where(c // SB_HEAD_DIM == h, acc_ref[h * t_new:(h + 1) * t_new, :], out)
        o_ref[0] = out.astype(o_ref.dtype)


def _sb_sample(page_table, bias, q, k_new, v_new, cache_k, cache_v, pages_per_step):
    b, t_new, _ = q.shape
    n_pages = page_table.shape[1]
    assert n_pages % pages_per_step == 0 and t_new % SUBLANES == 0
    steps = n_pages // pages_per_step
    rows = SB_HEADS * t_new

    def page_spec(p):
        return pl.BlockSpec((1, SB_HEADS, SB_HEAD_DIM, PAGE_SIZE),
                            lambda bi, g, pt: (pt[bi, n_pages - 1 - (g * pages_per_step + p)], 0, 0, 0))

    tok = lambda dt: pl.BlockSpec((1, t_new, SB_WIDTH), lambda bi, g, pt: (bi, 0, 0))
    grid_spec = pltpu.PrefetchScalarGridSpec(
        num_scalar_prefetch=1,
        grid=(b, steps),
        in_specs=[pl.BlockSpec(memory_space=pltpu.SMEM), tok(BF16), tok(F32), tok(F32)]
        + [page_spec(p) for p in range(pages_per_step)] * 2,
        out_specs=pl.BlockSpec((1, t_new, SB_WIDTH), lambda bi, g, pt: (bi, 0, 0)),
        scratch_shapes=[pltpu.VMEM((rows, SB_WIDTH), BF16),
                        pltpu.VMEM((rows, PAGE_SIZE), F32),
                        pltpu.VMEM((rows, PAGE_SIZE), F32),
                        pltpu.VMEM((rows, SB_WIDTH), F32)],
    )
    return pl.pallas_call(
        functools.partial(_sb_sample_kernel, pages_per_step=pages_per_step, t_new=t_new),
        grid_spec=grid_spec,
        out_shape=jax.ShapeDtypeStruct((b, t_new, SB_WIDTH), F32),
        compiler_params=_compiler_params(("parallel", "arbitrary")),
        name="sb_sample",
    )(page_table, bias, q, k_new, v_new,
      *([cache_k] * pages_per_step), *([cache_v] * pages_per_step))


def _layer_norm(x, g, b):
    mu = jnp.mean(x, axis=-1, keepdims=True)
    xc = x - mu
    var = jnp.mean(xc * xc, axis=-1, keepdims=True)
    return xc * lax.rsqrt(var + LN_EPS) * g + b


def _mix_kernel(x_ref, attn_ref, u_ref, prev_ref, wpool_ref, pscale_ref, wout_ref, g_ref, b_ref, o_ref,
                *, alpha, pos0, zero_first_prev):
    bb, t, _ = u_ref.shape
    i = pl.program_id(1)
    prev = prev_ref[...]
    if zero_first_prev:
        prev = jnp.where(i == 0, 0.0, prev)
    ext = jnp.concatenate([prev, u_ref[...]], axis=1)
    pos = pos0 + i * t + lax.broadcasted_iota(jnp.int32, (1, t, 1), 1)
    mixed = []
    for gi, win in enumerate(POOL_WINDOWS):
        sl = slice(gi * POOL_GROUP_DIM, (gi + 1) * POOL_GROUP_DIM)
        acc = ext[:, :, sl]
        span = 1
        while span < win:
            acc = acc[:, span:, :] + acc[:, :acc.shape[1] - span, :]
            span *= 2
        tok = ext[:, POOL_PREV_ROWS:, sl]
        inv = 1.0 / jnp.minimum(win, pos + 1).astype(F32)
        pooled = acc[:, acc.shape[1] - t:, :] * inv - tok
        pooled = pooled.reshape(bb * t, POOL_GROUP_DIM).astype(BF16)
        mixed.append(jnp.dot(pooled, wpool_ref[gi], preferred_element_type=F32))
    pool_out = (jnp.concatenate(mixed, axis=-1) * pscale_ref[...]).astype(BF16)
    attn = attn_ref[...].reshape(bb * t, SB_WIDTH).astype(BF16)
    mix = (jnp.dot(attn, wout_ref[0:SB_WIDTH, :], preferred_element_type=F32)
           + jnp.dot(pool_out, wout_ref[SB_WIDTH:, :], preferred_element_type=F32))
    x = x_ref[...].reshape(bb * t, x_ref.shape[2])
    o_ref[...] = _layer_norm(alpha * x + mix, g_ref[...], b_ref[...]).reshape(o_ref.shape)


def _mix(x, attn, u, prev, wpool_bf, pscale, wout_bf, ln_g, ln_b, *, bb, t, alpha, pos0, zero_first_prev):
    b, s, d = x.shape
    assert b % bb == 0 and s % t == 0 and t % SUBLANES == 0
    if zero_first_prev:
        assert t % POOL_PREV_ROWS == 0
        prev_map = lambda bi, i: (bi, jnp.maximum(i * (t // POOL_PREV_ROWS) - 1, 0), 0)
    else:
        assert s == t
        prev_map = lambda bi, i: (bi, 0, 0)
    tile = lambda w: pl.BlockSpec((bb, t, w), lambda bi, i: (bi, i, 0))
    const2 = lambda shape: pl.BlockSpec(shape, lambda bi, i: (0, 0))
    return pl.pallas_call(
        functools.partial(_mix_kernel, alpha=alpha, pos0=pos0, zero_first_prev=zero_first_prev),
        grid=(b // bb, s // t),
        in_specs=[tile(d), tile(SB_WIDTH), tile(POOL_WIDTH),
                  pl.BlockSpec((bb, POOL_PREV_ROWS, POOL_WIDTH), prev_map),
                  pl.BlockSpec(wpool_bf.shape, lambda bi, i: (0, 0, 0)),
                  const2((1, POOL_WIDTH)), const2(wout_bf.shape), const2((1, d)), const2((1, d))],
        out_specs=tile(d),
        out_shape=jax.ShapeDtypeStruct((b, s, d), F32),
        compiler_params=_compiler_params(("parallel", "arbitrary")),
        name="mix_ln1",
    )(x, attn, u, prev, wpool_bf, pscale, wout_bf, ln_g, ln_b)


def _sorting_network(n):
    pairs = []

    def merge(lo, m, r):
        step = 2 * r
        if step < m:
            merge(lo, m, step)
            merge(lo + r, m, step)
            pairs.extend((i, i + r) for i in range(lo + r, lo + m - r, step))
        else:
            pairs.append((lo, lo + r))

    def sort(lo, m):
        if m > 1:
            sort(lo, m // 2)
            sort(lo + m // 2, m // 2)
            merge(lo, m, 1)

    sort(0, n)
    return pairs


def _top_values(s, count):
    n = s.shape[0] // SUBLANES
    v = [s[SUBLANES * k:SUBLANES * (k + 1), :] for k in range(n)]
    for i, j in _sorting_network(n):
        v[i], v[j] = jnp.maximum(v[i], v[j]), jnp.minimum(v[i], v[j])
    ridx = lax.broadcasted_iota(jnp.int32, (count, s.shape[1]), 0)
    top = jnp.full((count, s.shape[1]), NEG_BIG, F32)
    for r in range(count):
        m = jnp.max(v[0], axis=0, keepdims=True)
        top = jnp.where(ridx == r, m, top)
        hit = v[0] == m
        need = count - r - 1
        for k in range(min(need, n - 1)):
            v[k] = jnp.where(hit, v[k + 1], v[k])
        if need >= n:
            v[n - 1] = jnp.where(hit, NEG_BIG, v[n - 1])
    return top


_PACKED_PAIRS = [(k, kp) for k in range(2, PEER_TOPK) for kp in range(PEER_TOPK // (k + 1))]


def _candidate_sums(a, b):
    t = a.shape[1]
    row = lax.broadcasted_iota(jnp.int32, (SUBLANES, t), 0)
    groups = [a[0:1, :] + b, a[1:2, :] + b[0:SUBLANES, :]]
    for g in range(0, len(_PACKED_PAIRS), SUBLANES):
        a_pat = jnp.full((SUBLANES, t), NEG_BIG, F32)
        b_pat = jnp.zeros((SUBLANES, t), F32)
        for r, (k, kp) in enumerate(_PACKED_PAIRS[g:g + SUBLANES]):
            a_pat = jnp.where(row == r, a[k:k + 1, :], a_pat)
            b_pat = jnp.where(row == r, b[kp:kp + 1, :], b_pat)
        groups.append(a_pat + b_pat)
    n_groups = sum(g.shape[0] for g in groups) // SUBLANES
    pad = (1 << (n_groups - 1).bit_length()) - n_groups
    groups += [jnp.full((SUBLANES, t), NEG_BIG, F32)] * pad
    return jnp.concatenate(groups, axis=0)


def _route_tile(s1, s2):
    k_top = PEER_TOPK
    a = _top_values(s1, k_top)
    b = _top_values(s2, k_top)
    cv = _top_values(_candidate_sums(a, b), k_top)
    least = cv[k_top - 1:k_top, :]
    z = jnp.sum(jnp.exp(cv - cv[0:1, :]), axis=0, keepdims=True)
    rank2 = jnp.full_like(s2, float(k_top))
    for k in reversed(range(k_top)):
        rank2 = jnp.where(s2 >= b[k:k + 1, :], float(k), rank2)
    cnt = jnp.zeros_like(s1)
    for k in range(SUBLANES):
        cnt = jnp.where(s1 + b[k:k + 1, :] >= least, float(k + 1), cnt)
    tail = jnp.sum(jnp.where(a[0:1, :] + b[SUBLANES:, :] >= least, 1.0, 0.0), axis=0, keepdims=True)
    cnt = cnt + jnp.where(s1 == a[0:1, :], tail, 0.0)
    return rank2, jnp.exp(s2 - b[0:1, :]), cnt, jnp.exp(s1 - a[0:1, :]) * (0.5 / z)


def _score_weights_kernel(keys_ref, wq_ref, o_ref):
    o_ref[...] = lax.dot_general(keys_ref[0], wq_ref[...], _NT, precision=lax.Precision.HIGHEST,
                                 preferred_element_type=F32).astype(o_ref.dtype)


def _score_weights(keys, w_query):
    n_hp, n_keys, half = keys.shape
    d = w_query.shape[0]
    return pl.pallas_call(
        _score_weights_kernel,
        grid=(n_hp,),
        in_specs=[pl.BlockSpec((1, n_keys, half), lambda i: (i, 0, 0)),
                  pl.BlockSpec((d, half), lambda i: (0, i))],
        out_specs=pl.BlockSpec((n_keys, d), lambda i: (i, 0)),
        out_shape=jax.ShapeDtypeStruct((n_hp * n_keys, d), BF16),
        compiler_params=_compiler_params(("parallel",)),
        name="peer_score_weights",
    )(keys, w_query)


def _peer_route_kernel(x0_ref, xn_ref, w_ref, rk_ref, eb_ref, cnt_ref, ea_ref, s_ref):
    i = pl.program_id(0)

    def scores(x_ref, slot):
        s_ref[slot] = lax.dot_general(w_ref[...], x_ref[...].astype(BF16), _NT, preferred_element_type=F32)

    @pl.when(i == 0)
    def _():
        scores(x0_ref, 0)

    for cur in range(2):
        @pl.when(i % 2 == cur)
        def _():
            scores(xn_ref, 1 - cur)
            for h in range(PEER_HEADS):
                r1 = 2 * h * PEER_N_KEYS
                r2 = r1 + PEER_N_KEYS
                for lt in range(rk_ref.shape[2] // LANES):
                    sl = slice(lt * LANES, (lt + 1) * LANES)
                    rank2, eb, cnt, ea = _route_tile(s_ref[cur, r1:r2, sl], s_ref[cur, r2:r2 + PEER_N_KEYS, sl])
                    rk_ref[h, :, sl] = rank2.astype(BF16)
                    eb_ref[h, :, sl] = eb.astype(BF16)
                    cnt_ref[h, :, sl] = cnt
                    ea_ref[h, :, sl] = ea


def _peer_route(x1, w_scores, tr):
    n, d = x1.shape
    assert n % tr == 0 and tr % LANES == 0
    steps = n // tr
    out_block = pl.BlockSpec((PEER_HEADS, PEER_N_KEYS, tr), lambda i: (0, 0, i))
    shape = (PEER_HEADS, PEER_N_KEYS, n)
    return pl.pallas_call(
        _peer_route_kernel,
        grid=(steps,),
        in_specs=[pl.BlockSpec((tr, d), lambda i: (0, 0)),
                  pl.BlockSpec((tr, d), lambda i: (jnp.minimum(i + 1, steps - 1), 0)),
                  pl.BlockSpec(w_scores.shape, lambda i: (0, 0))],
        out_specs=[out_block] * 4,
        out_shape=[jax.ShapeDtypeStruct(shape, BF16), jax.ShapeDtypeStruct(shape, BF16),
                   jax.ShapeDtypeStruct(shape, F32), jax.ShapeDtypeStruct(shape, F32)],
        scratch_shapes=[pltpu.VMEM((2, w_scores.shape[0], tr), F32)],
        compiler_params=_compiler_params(("arbitrary",)),
        name="peer_route",
    )(x1, x1, w_scores)


def _peer_chunk(act_ref, i0, vt_ref, rk_ref, eb_ref, cnt_ref, ea_ref, rows_per_chunk):
    tm = act_ref.shape[1]
    tile = (BF16_TILE_ROWS, tm)
    tiles = (PEER_N_KEYS // BF16_TILE_ROWS, BF16_TILE_ROWS, tm)
    total = None
    for pair in range(rows_per_chunk // 2):
        rows = (2 * pair, 2 * pair + 1)
        gates = [None, None]
        for h in range(PEER_HEADS):
            rk = rk_ref[h].reshape(tiles)
            eb = eb_ref[h].reshape(tiles)
            for slot, ii in enumerate(rows):
                cnt_i = jnp.broadcast_to(cnt_ref[h, pl.ds(i0 + ii, 1), :], tile).astype(BF16)
                ea_i = jnp.broadcast_to(ea_ref[h, pl.ds(i0 + ii, 1), :], tile).astype(BF16)
                term = jnp.where(rk < cnt_i[None], ea_i[None] * eb, jnp.zeros_like(eb))
                gates[slot] = term if gates[slot] is None else gates[slot] + term
        coef = []
        for gate, ii in zip(gates, rows):
            a = act_ref[ii * PEER_N_KEYS:(ii + 1) * PEER_N_KEYS, :]
            gelu2 = a * (1.0 + lax.erf(a * (2.0 ** -0.5)))
            coef.append(gate.reshape(PEER_N_KEYS, tm) * gelu2.astype(BF16))
        lo = 2 * pair * PEER_N_KEYS
        part = jnp.dot(vt_ref[0, :, lo:lo + 2 * PEER_N_KEYS], jnp.concatenate(coef, axis=0),
                       preferred_element_type=F32)
        total = part if total is None else total + part
    return total


def _peer_kernel(x_ref, u0_ref, un_ref, vt_ref, rk_ref, eb_ref, cnt_ref, ea_ref, g_ref, b_ref, o_ref,
                 xb_ref, act_ref, acc_ref, *, alpha, rows_per_chunk):
    c = pl.program_id(1)
    route = (rk_ref, eb_ref, cnt_ref, ea_ref)

    @pl.when(c == 0)
    def _():
        xb_ref[...] = x_ref[...].T.astype(BF16)
        act_ref[0] = jnp.dot(u0_ref[...], xb_ref[...], preferred_element_type=F32)
        acc_ref[...] = jnp.zeros_like(acc_ref)

    i0 = pl.multiple_of(c * rows_per_chunk, rows_per_chunk)
    for cur in range(2):
        @pl.when(c % 2 == cur)
        def _():
            act_ref[1 - cur] = jnp.dot(un_ref[...], xb_ref[...], preferred_element_type=F32)
            acc_ref[...] += _peer_chunk(act_ref.at[cur], i0, vt_ref, *route, rows_per_chunk)

    @pl.when(c == pl.num_programs(1) - 1)
    def _():
        peer = acc_ref[...].T
        o_ref[...] = _layer_norm(alpha * x_ref[...] + peer, g_ref[...], b_ref[...])


def _peer(x1, w_scores, u_bf, vt_bf, ln_g, ln_b, *, tr, tm, rows_per_chunk, alpha):
    n, d = x1.shape
    n_exp = u_bf.shape[0]
    chunk = rows_per_chunk * PEER_N_KEYS
    assert n % tm == 0 and n_exp % (2 * chunk) == 0 and rows_per_chunk % SUBLANES == 0
    n_chunks = n_exp // chunk
    routed = _peer_route(x1, w_scores, tr)
    const = lambda shape: pl.BlockSpec(shape, lambda i, c: (0,) * len(shape))
    route_block = pl.BlockSpec((PEER_HEADS, PEER_N_KEYS, tm), lambda i, c: (0, 0, i))
    return pl.pallas_call(
        functools.partial(_peer_kernel, alpha=alpha, rows_per_chunk=rows_per_chunk),
        grid=(n // tm, n_chunks),
        in_specs=[pl.BlockSpec((tm, d), lambda i, c: (i, 0)),
                  pl.BlockSpec((chunk, d), lambda i, c: (0, 0)),
                  pl.BlockSpec((chunk, d), lambda i, c: (jnp.minimum(c + 1, n_chunks - 1), 0)),
                  pl.BlockSpec((1, d, chunk), lambda i, c: (c, 0, 0)),
                  route_block, route_block, route_block, route_block,
                  const((1, d)), const((1, d))],
        out_specs=pl.BlockSpec((tm, d), lambda i, c: (i, 0)),
        out_shape=jax.ShapeDtypeStruct((n, d), F32),
        scratch_shapes=[pltpu.VMEM((d, tm), BF16), pltpu.VMEM((2, chunk, tm), F32), pltpu.VMEM((d, tm), F32)],
        compiler_params=_compiler_params(("parallel", "arbitrary")),
        name="peer_ln2",
    )(x1, u_bf, u_bf, vt_bf, *routed, ln_g, ln_b)


def _pick(n, prefs):
    for p in prefs:
        if n % p == 0:
            return p
    return n


def kernel(x_prompt, x_sample, cache_k, cache_v, state_pool, page_table, w_in, sb_bias, w_out, w_pool,
           pool_scale, ln1_g, ln1_b, w_query, sub_keys, u_table, v_table, ln2_g, ln2_b):
    depth = w_in.shape[0]
    b, s, d = x_prompt.shape
    db, t_new, _ = x_sample.shape
    n_pages = page_table.shape[1]
    past_len = n_pages * PAGE_SIZE
    alpha = (2.0 * depth) ** 0.25
    n_pool = cache_k.shape[1]

    hp, hs = x_prompt, x_sample
    outs = [[] for _ in range(6)]
    for l in range(depth):
        w_in_bf = w_in[l].astype(BF16)
        wout_bf = w_out[l].astype(BF16)
        wpool_bf = w_pool[l].astype(BF16)
        w_scores = _score_weights(sub_keys[l].reshape(2 * PEER_HEADS, PEER_N_KEYS, PEER_HALF), w_query[l])
        u_bf = u_table[l].astype(BF16)
        vt_bf = v_table[l].reshape(-1, PEER_CHUNK_ROWS * PEER_N_KEYS, d).transpose(0, 2, 1).astype(BF16)
        pscale = pool_scale[l].reshape(1, POOL_WIDTH)
        g1, b1 = ln1_g[l].reshape(1, d), ln1_b[l].reshape(1, d)
        g2, b2 = ln2_g[l].reshape(1, d), ln2_b[l].reshape(1, d)
        bias = sb_bias[l].astype(F32)
        peer = functools.partial(_peer, w_scores=w_scores, u_bf=u_bf, vt_bf=vt_bf,
                                 ln_g=g2, ln_b=b2, rows_per_chunk=PEER_CHUNK_ROWS, alpha=alpha)

        n = b * s
        q, kt, vt, u, kb, vb = _in_proj(hp.reshape(n, d), w_in_bf, _pick(s, (512, 256, 128)), seq=s)
        attn = _sb_prompt(bias, q.reshape(b, s, SB_WIDTH), kb.reshape(b, s, SB_WIDTH),
                          vb.reshape(b, s, SB_WIDTH), _pick(s, (256, 128)))
        u3 = u.reshape(b, s, POOL_WIDTH)
        x1 = _mix(hp, attn, u3, u3, wpool_bf, pscale, wout_bf, g1, b1,
                  bb=1, t=_pick(s, (512, 256, 128)), alpha=alpha, pos0=0, zero_first_prev=True)
        hp = peer(x1.reshape(n, d), tr=_pick(n, (256, 128)), tm=_pick(n, (512, 256, 128))).reshape(b, s, d)
        outs[0].append(kt.reshape(b, SB_HEADS, SB_HEAD_DIM, s).transpose(0, 3, 1, 2))
        outs[1].append(vt.reshape(b, SB_HEADS, SB_HEAD_DIM, s).transpose(0, 3, 1, 2))
        outs[2].append(u3[:, s - POOL_STATE_LEN:])

        ns = db * t_new
        qs, ks, vs, us, _, _ = _in_proj(hs.reshape(ns, d), w_in_bf, _pick(ns, (512, 256, 128)))
        attn_s = _sb_sample(page_table, bias, qs.reshape(db, t_new, SB_WIDTH),
                            ks.reshape(db, t_new, SB_WIDTH), vs.reshape(db, t_new, SB_WIDTH),
                            cache_k[l].transpose(0, 2, 3, 1), cache_v[l].transpose(0, 2, 3, 1),
                            _pick(n_pages, (32, 16, 8, 4, 2, 1)))
        us3 = us.reshape(db, t_new, POOL_WIDTH)
        state = state_pool[l].astype(F32)
        prev = jnp.pad(state, ((0, 0), (POOL_PREV_ROWS - POOL_STATE_LEN, 0), (0, 0)))
        x1s = _mix(hs, attn_s, us3, prev, wpool_bf, pscale, wout_bf, g1, b1,
                   bb=_pick(db, (16, 8, 4, 2, 1)), t=t_new, alpha=alpha, pos0=past_len, zero_first_prev=False)
        hs = peer(x1s.reshape(ns, d), tr=_pick(ns, (256, 128)), tm=_pick(ns, (512, 256, 128))).reshape(db, t_new, d)
        outs[3].append(ks.reshape(db, t_new, SB_HEADS, SB_HEAD_DIM))
        outs[4].append(vs.reshape(db, t_new, SB_HEADS, SB_HEAD_DIM))
        outs[5].append(jnp.concatenate([state, us3], axis=1)[:, -POOL_STATE_LEN:])

    return (hp, hs, jnp.stack(outs[0]), jnp.stack(outs[1]), jnp.stack(outs[2]),
            jnp.stack(outs[3]), jnp.stack(outs[4]), jnp.stack(outs[5]))
```

```python
import functools
import math

import jax
import jax.numpy as jnp
from jax import lax
from jax.experimental import pallas as pl
from jax.experimental.pallas import tpu as pltpu

F32 = jnp.float32
BF16 = jnp.bfloat16

SB_HEADS = 8
SB_HEAD_DIM = 64
SB_WIDTH = SB_HEADS * SB_HEAD_DIM
POOL_WINDOWS = (2, 4, 8, 16)
POOL_GROUP_DIM = 128
POOL_WIDTH = len(POOL_WINDOWS) * POOL_GROUP_DIM
POOL_STATE_LEN = max(POOL_WINDOWS) - 1
POOL_PREV_ROWS = 16
PAGE_SIZE = 128
PEER_HEADS = 8
PEER_N_KEYS = 128
PEER_TOPK = 16
PEER_HALF = 128
PEER_CHUNK_ROWS = 8
LN_EPS = 1e-5

LANES = 128
SUBLANES = 8
BF16_TILE_ROWS = 16
VMEM_LIMIT_BYTES = 56 * 1024 * 1024

NEG_BIG = -1e30
LOG2_E = 1.4426950408889634
_NT = (((1,), (1,)), ((), ()))


def _compiler_params(semantics):
    return pltpu.CompilerParams(dimension_semantics=semantics, vmem_limit_bytes=VMEM_LIMIT_BYTES)


def _proj_kernel(x_ref, w_ref, q_ref, k_ref, v_ref, u_ref, kb_ref, vb_ref, *, tokens_last):
    xb = x_ref[...].astype(BF16)
    w = SB_WIDTH
    q = jnp.dot(xb, w_ref[:, 0:w], preferred_element_type=F32)
    q_ref[...] = (q * -(SB_HEAD_DIM ** -0.5)).astype(BF16)
    k = jnp.dot(xb, w_ref[:, w:2 * w], preferred_element_type=F32)
    kb_ref[...] = k.astype(BF16)
    v = jnp.dot(xb, w_ref[:, 2 * w:3 * w], preferred_element_type=F32)
    vb_ref[...] = v.astype(BF16)
    u_ref[...] = jnp.dot(xb, w_ref[:, 3 * w:], preferred_element_type=F32)
    if tokens_last:
        k_ref[0] = k.T
        v_ref[0] = v.T
    else:
        k_ref[...] = k
        v_ref[...] = v


def _in_proj(x2d, w_in_bf, tm, seq=None):
    n, d = x2d.shape
    pw = w_in_bf.shape[1]
    row = lambda i: (i, 0)
    out_block = pl.BlockSpec((tm, SB_WIDTH), row)
    if seq is None:
        kv_block, kv_shape = out_block, (n, SB_WIDTH)
    else:
        assert seq % tm == 0 and n % seq == 0
        per_seq = seq // tm
        kv_block = pl.BlockSpec((1, SB_WIDTH, tm), lambda i: (i // per_seq, 0, i % per_seq))
        kv_shape = (n // seq, SB_WIDTH, seq)
    return pl.pallas_call(
        functools.partial(_proj_kernel, tokens_last=seq is not None),
        grid=(n // tm,),
        in_specs=[pl.BlockSpec((tm, d), row), pl.BlockSpec((d, pw), lambda i: (0, 0))],
        out_specs=[out_block, kv_block, kv_block, out_block, out_block, out_block],
        out_shape=[jax.ShapeDtypeStruct((n, SB_WIDTH), BF16),
                   jax.ShapeDtypeStruct(kv_shape, F32),
                   jax.ShapeDtypeStruct(kv_shape, F32),
                   jax.ShapeDtypeStruct((n, POOL_WIDTH), F32),
                   jax.ShapeDtypeStruct((n, SB_WIDTH), BF16),
                   jax.ShapeDtypeStruct((n, SB_WIDTH), BF16)],
        compiler_params=_compiler_params(("parallel",)),
        name="in_proj",
    )(x2d, w_in_bf)


def _suffix_sum_matrix(tk):
    r = lax.broadcasted_iota(jnp.int32, (tk, 2 * tk), 0)
    c = lax.broadcasted_iota(jnp.int32, (tk, 2 * tk), 1)
    return jnp.where((r > c) | (c >= tk), 1.0, 0.0).astype(BF16)


def _log_one_minus_beta(nz):
    neg_abs = lax.bitcast_convert_type(lax.bitcast_convert_type(nz, jnp.uint32) | jnp.uint32(1 << 31), F32)
    return jnp.minimum(nz, 0.0) - jnp.log(1.0 + jnp.exp2(neg_abs * LOG2_E))


def _sb_weights(nzs, carry, tmat, causal):
    rows, tk = nzs[0].shape
    log1m = [_log_one_minus_beta(nz) for nz in nzs]
    if causal is not None:
        log1m = [jnp.where(causal, l, 0.0) for l in log1m]
    sums = jnp.dot(jnp.concatenate(log1m, axis=0).astype(BF16), tmat, preferred_element_type=F32)
    ws = []
    for p, (nz, l) in enumerate(zip(nzs, log1m)):
        sp = sums[p * rows:(p + 1) * rows]
        w = jnp.exp((l - nz) + (sp[:, :tk] + carry))
        if causal is not None:
            w = jnp.where(causal, w, 0.0)
        ws.append(w.astype(BF16))
        carry = carry + sp[:, tk:]
    return ws, carry


def _sb_prompt_kernel(bias_ref, q_ref, k_ref, v_ref, o_ref,
                      qs_ref, zl_ref, l_ref, w_ref, carry_ref, acc_ref, *, tq):
    qi = pl.program_id(1)
    n_pairs = SB_HEADS // 2
    r = lax.broadcasted_iota(jnp.int32, (tq, tq), 0)
    c = lax.broadcasted_iota(jnp.int32, (tq, tq), 1)
    strict = c < r
    tmat = jnp.where(r > c, 1.0, 0.0).astype(BF16)
    lane = lax.broadcasted_iota(jnp.int32, (tq, LANES), 1)
    low = lane < SB_HEAD_DIM
    for hp in range(n_pairs):
        q = q_ref[0, :, hp * LANES:(hp + 1) * LANES]
        zero = jnp.zeros_like(q)
        qs_ref[hp, 0:tq, :] = jnp.where(low, q, zero)
        qs_ref[hp, tq:2 * tq, :] = jnp.where(low, zero, q)
    carry_ref[...] = jnp.zeros_like(carry_ref)
    acc_ref[...] = jnp.zeros_like(acc_ref)

    def block(j, causal):
        start = pl.multiple_of(j * tq, tq)
        for hp in range(n_pairs):
            kj = k_ref[0, pl.ds(start, tq), hp * LANES:(hp + 1) * LANES]
            z2 = lax.dot_general(qs_ref[hp], kj, (((1,), (1,)), ((), ())), preferred_element_type=F32)
            for h2 in range(2):
                h = 2 * hp + h2
                nz = z2[h2 * tq:(h2 + 1) * tq, :] - bias_ref[h]
                log1m = _log_one_minus_beta(nz)
                if causal:
                    log1m = jnp.where(strict, log1m, 0.0)
                zl_ref[h] = log1m - nz
                l_ref[h * tq:(h + 1) * tq, :] = log1m.astype(BF16)
        later = jnp.dot(l_ref[...], tmat, preferred_element_type=F32)
        for hp in range(n_pairs):
            for h2 in range(2):
                h = 2 * hp + h2
                rest = later[h * tq:(h + 1) * tq, :]
                first = l_ref[h * tq:(h + 1) * tq, 0:1].astype(F32)
                rowsum = jnp.broadcast_to(rest[:, 0:1] + first, (tq, LANES))
                carry = carry_ref[h]
                w = jnp.exp(zl_ref[h] + rest + jnp.concatenate([carry] * (tq // LANES), axis=1))
                if causal:
                    w = jnp.where(strict, w, 0.0)
                w_ref[hp, h2 * tq:(h2 + 1) * tq, :] = w.astype(BF16)
                carry_ref[h] = carry + rowsum
        for hp in range(n_pairs):
            vj = v_ref[0, pl.ds(start, tq), hp * LANES:(hp + 1) * LANES]
            acc_ref[hp] += jnp.dot(w_ref[hp], vj, preferred_element_type=F32)

    block(qi, True)

    def body(it, _):
        block(qi - 1 - it, False)
        return 0

    lax.fori_loop(0, qi, body, 0)
    for hp in range(n_pairs):
        o_ref[0, :, hp * LANES:(hp + 1) * LANES] = jnp.where(
            low, acc_ref[hp, 0:tq, :], acc_ref[hp, tq:2 * tq, :]).astype(o_ref.dtype)


def _sb_prompt(bias, q, kb, vb, tq):
    b, s, _ = q.shape
    assert tq % LANES == 0 and s % tq == 0
    qspec = pl.BlockSpec((1, tq, SB_WIDTH), lambda bi, qi: (bi, qi, 0))
    kvspec = pl.BlockSpec((1, s, SB_WIDTH), lambda bi, qi: (bi, 0, 0))
    n_pairs = SB_HEADS // 2
    return pl.pallas_call(
        functools.partial(_sb_prompt_kernel, tq=tq),
        grid=(b, s // tq),
        in_specs=[pl.BlockSpec(memory_space=pltpu.SMEM), qspec, kvspec, kvspec],
        out_specs=qspec,
        out_shape=jax.ShapeDtypeStruct((b, s, SB_WIDTH), BF16),
        scratch_shapes=[pltpu.VMEM((n_pairs, 2 * tq, LANES), BF16),
                        pltpu.VMEM((SB_HEADS, tq, tq), F32),
                        pltpu.VMEM((SB_HEADS * tq, tq), BF16),
                        pltpu.VMEM((n_pairs, 2 * tq, tq), BF16),
                        pltpu.VMEM((SB_HEADS, tq, LANES), F32),
                        pltpu.VMEM((n_pairs, 2 * tq, LANES), F32)],
        compiler_params=_compiler_params(("parallel", "arbitrary")),
        name="sb_prompt",
    )(bias, q, kb, vb)


def _sb_sample_kernel(pt_ref, bias_ref, q_ref, kn_ref, vn_ref, *rest, pages_per_step, t_new):
    del pt_ref
    kpages = rest[:pages_per_step]
    vpages = rest[pages_per_step:2 * pages_per_step]
    o_ref = rest[2 * pages_per_step]
    qbd_ref, biasrow_ref, carry_ref, acc_ref = rest[2 * pages_per_step + 1:]
    g = pl.program_id(1)
    rows = SB_HEADS * t_new
    tmat = _suffix_sum_matrix(PAGE_SIZE)

    def weights(zs, causal):
        ws, c = _sb_weights([z - biasrow_ref[...] for z in zs], carry_ref[...], tmat, causal)
        carry_ref[...] = c
        return ws

    @pl.when(g == 0)
    def _():
        r = lax.broadcasted_iota(jnp.int32, (rows, SB_WIDTH), 0)
        c = lax.broadcasted_iota(jnp.int32, (rows, SB_WIDTH), 1)
        qrep = jnp.concatenate([q_ref[0].astype(F32)] * SB_HEADS, axis=0)
        qbd_ref[...] = jnp.where(c // SB_HEAD_DIM == r // t_new, qrep, 0.0).astype(BF16)
        rr = lax.broadcasted_iota(jnp.int32, (rows, PAGE_SIZE), 0)
        brow = jnp.zeros((rows, PAGE_SIZE), F32)
        for h in range(SB_HEADS):
            brow = jnp.where(rr // t_new == h, bias_ref[h], brow)
        biasrow_ref[...] = brow
        carry_ref[...] = jnp.zeros_like(carry_ref)
        acc_ref[...] = jnp.zeros_like(acc_ref)
        pad = jnp.zeros((PAGE_SIZE - t_new, SB_WIDTH), F32)
        kn = jnp.concatenate([kn_ref[0], pad], axis=0).astype(BF16)
        vn = jnp.concatenate([vn_ref[0], pad], axis=0).astype(BF16)
        cc = lax.broadcasted_iota(jnp.int32, (rows, PAGE_SIZE), 1)
        w_new, = weights([lax.dot_general(qbd_ref[...], kn, _NT, preferred_element_type=F32)],
                         cc < rr % t_new)
        acc_ref[...] += jnp.dot(w_new, vn, preferred_element_type=F32)

    qbd = qbd_ref[...]
    ws = weights([jnp.dot(qbd, kp[0].reshape(SB_WIDTH, PAGE_SIZE).astype(BF16), preferred_element_type=F32)
                  for kp in kpages], None)
    vt = jnp.concatenate([vp[0].reshape(SB_WIDTH, PAGE_SIZE).astype(BF16) for vp in vpages], axis=1)
    acc_ref[...] += lax.dot_general(jnp.concatenate(ws, axis=1), vt, _NT, preferred_element_type=F32)

    @pl.when(g == pl.num_programs(1) - 1)
    def _():
        c = lax.broadcasted_iota(jnp.int32, (t_new, SB_WIDTH), 1)
        out = jnp.zeros((t_new, SB_WIDTH), F32)
        for h in range(SB_HEADS):
            out = jnp.where(c // SB_HEAD_DIM == h, acc_ref[h * t_new:(h + 1) * t_new, :], out)
        o_ref[0] = out.astype(o_ref.dtype)


def _sb_sample(page_table, bias, q, k_new, v_new, cache_k, cache_v, pages_per_step):
    b, t_new, _ = q.shape
    n_pages = page_table.shape[1]
    assert n_pages % pages_per_step == 0 and t_new % SUBLANES == 0
    steps = n_pages // pages_per_step
    rows = SB_HEADS * t_new

    def page_spec(p):
        return pl.BlockSpec((1, SB_HEADS, SB_HEAD_DIM, PAGE_SIZE),
                            lambda bi, g, pt: (pt[bi, n_pages - 1 - (g * pages_per_step + p)], 0, 0, 0))

    tok = lambda dt: pl.BlockSpec((1, t_new, SB_WIDTH), lambda bi, g, pt: (bi, 0, 0))
    grid_spec = pltpu.PrefetchScalarGridSpec(
        num_scalar_prefetch=1,
        grid=(b, steps),
        in_specs=[pl.BlockSpec(memory_space=pltpu.SMEM), tok(BF16), tok(F32), tok(F32)]
        + [page_spec(p) for p in range(pages_per_step)] * 2,
        out_specs=pl.BlockSpec((1, t_new, SB_WIDTH), lambda bi, g, pt: (bi, 0, 0)),
        scratch_shapes=[pltpu.VMEM((rows, SB_WIDTH), BF16),
                        pltpu.VMEM((rows, PAGE_SIZE), F32),
                        pltpu.VMEM((rows, PAGE_SIZE), F32),
                        pltpu.VMEM((rows, SB_WIDTH), F32)],
    )
    return pl.pallas_call(
        functools.partial(_sb_sample_kernel, pages_per_step=pages_per_step, t_new=t_new),
        grid_spec=grid_spec,
        out_shape=jax.ShapeDtypeStruct((b, t_new, SB_WIDTH), F32),
        compiler_params=_compiler_params(("parallel", "arbitrary")),
        name="sb_sample",
    )(page_table, bias, q, k_new, v_new,
      *([cache_k] * pages_per_step), *([cache_v] * pages_per_step))


def _layer_norm(x, g, b):
    mu = jnp.mean(x, axis=-1, keepdims=True)
    xc = x - mu
    var = jnp.mean(xc * xc, axis=-1, keepdims=True)
    return xc * lax.rsqrt(var + LN_EPS) * g + b


def _mix_kernel(x_ref, attn_ref, u_ref, prev_ref, wpool_ref, pscale_ref, wout_ref, g_ref, b_ref, o_ref,
                *, alpha, pos0, zero_first_prev):
    bb, t, _ = u_ref.shape
    i = pl.program_id(1)
    prev = prev_ref[...]
    if zero_first_prev:
        prev = jnp.where(i == 0, 0.0, prev)
    ext = jnp.concatenate([prev, u_ref[...]], axis=1)
    pos = pos0 + i * t + lax.broadcasted_iota(jnp.int32, (1, t, 1), 1)
    mixed = []
    for gi, win in enumerate(POOL_WINDOWS):
        sl = slice(gi * POOL_GROUP_DIM, (gi + 1) * POOL_GROUP_DIM)
        acc = ext[:, :, sl]
        span = 1
        while span < win:
            acc = acc[:, span:, :] + acc[:, :acc.shape[1] - span, :]
            span *= 2
        tok = ext[:, POOL_PREV_ROWS:, sl]
        inv = 1.0 / jnp.minimum(win, pos + 1).astype(F32)
        pooled = acc[:, acc.shape[1] - t:, :] * inv - tok
        pooled = pooled.reshape(bb * t, POOL_GROUP_DIM).astype(BF16)
        mixed.append(jnp.dot(pooled, wpool_ref[gi], preferred_element_type=F32))
    pool_out = (jnp.concatenate(mixed, axis=-1) * pscale_ref[...]).astype(BF16)
    attn = attn_ref[...].reshape(bb * t, SB_WIDTH).astype(BF16)
    mix = (jnp.dot(attn, wout_ref[0:SB_WIDTH, :], preferred_element_type=F32)
           + jnp.dot(pool_out, wout_ref[SB_WIDTH:, :], preferred_element_type=F32))
    x = x_ref[...].reshape(bb * t, x_ref.shape[2])
    o_ref[...] = _layer_norm(alpha * x + mix, g_ref[...], b_ref[...]).reshape(o_ref.shape)


def _mix(x, attn, u, prev, wpool_bf, pscale, wout_bf, ln_g, ln_b, *, bb, t, alpha, pos0, zero_first_prev):
    b, s, d = x.shape
    assert b % bb == 0 and s % t == 0 and t % SUBLANES == 0
    if zero_first_prev:
        assert t % POOL_PREV_ROWS == 0
        prev_map = lambda bi, i: (bi, jnp.maximum(i * (t // POOL_PREV_ROWS) - 1, 0), 0)
    else:
        assert s == t
        prev_map = lambda bi, i: (bi, 0, 0)
    tile = lambda w: pl.BlockSpec((bb, t, w), lambda bi, i: (bi, i, 0))
    const2 = lambda shape: pl.BlockSpec(shape, lambda bi, i: (0, 0))
    return pl.pallas_call(
        functools.partial(_mix_kernel, alpha=alpha, pos0=pos0, zero_first_prev=zero_first_prev),
        grid=(b // bb, s // t),
        in_specs=[tile(d), tile(SB_WIDTH), tile(POOL_WIDTH),
                  pl.BlockSpec((bb, POOL_PREV_ROWS, POOL_WIDTH), prev_map),
                  pl.BlockSpec(wpool_bf.shape, lambda bi, i: (0, 0, 0)),
                  const2((1, POOL_WIDTH)), const2(wout_bf.shape), const2((1, d)), const2((1, d))],
        out_specs=tile(d),
        out_shape=jax.ShapeDtypeStruct((b, s, d), F32),
        compiler_params=_compiler_params(("parallel", "arbitrary")),
        name="mix_ln1",
    )(x, attn, u, prev, wpool_bf, pscale, wout_bf, ln_g, ln_b)


def _sorting_network(n):
    pairs = []

    def merge(lo, m, r):
        step = 2 * r
        if step < m:
            merge(lo, m, step)
            merge(lo + r, m, step)
            pairs.extend((i, i + r) for i in range(lo + r, lo + m - r, step))
        else:
            pairs.append((lo, lo + r))

    def sort(lo, m):
        if m > 1:
            sort(lo, m // 2)
            sort(lo + m // 2, m // 2)
            merge(lo, m, 1)

    sort(0, n)
    return pairs


def _top_values(s, count):
    n = s.shape[0] // SUBLANES
    v = [s[SUBLANES * k:SUBLANES * (k + 1), :] for k in range(n)]
    for i, j in _sorting_network(n):
        v[i], v[j] = jnp.maximum(v[i], v[j]), jnp.minimum(v[i], v[j])
    ridx = lax.broadcasted_iota(jnp.int32, (count, s.shape[1]), 0)
    top = jnp.full((count, s.shape[1]), NEG_BIG, F32)
    for r in range(count):
        m = jnp.max(v[0], axis=0, keepdims=True)
        top = jnp.where(ridx == r, m, top)
        hit = v[0] == m
        need = count - r - 1
        for k in range(min(need, n - 1)):
            v[k] = jnp.where(hit, v[k + 1], v[k])
        if need >= n:
            v[n - 1] = jnp.where(hit, NEG_BIG, v[n - 1])
    return top


_PACKED_PAIRS = [(k, kp) for k in range(2, PEER_TOPK) for kp in range(PEER_TOPK // (k + 1))]


def _candidate_sums(a, b):
    t = a.shape[1]
    row = lax.broadcasted_iota(jnp.int32, (SUBLANES, t), 0)
    groups = [a[0:1, :] + b, a[1:2, :] + b[0:SUBLANES, :]]
    for g in range(0, len(_PACKED_PAIRS), SUBLANES):
        a_pat = jnp.full((SUBLANES, t), NEG_BIG, F32)
        b_pat = jnp.zeros((SUBLANES, t), F32)
        for r, (k, kp) in enumerate(_PACKED_PAIRS[g:g + SUBLANES]):
            a_pat = jnp.where(row == r, a[k:k + 1, :], a_pat)
            b_pat = jnp.where(row == r, b[kp:kp + 1, :], b_pat)
        groups.append(a_pat + b_pat)
    n_groups = sum(g.shape[0] for g in groups) // SUBLANES
    pad = (1 << (n_groups - 1).bit_length()) - n_groups
    groups += [jnp.full((SUBLANES, t), NEG_BIG, F32)] * pad
    return jnp.concatenate(groups, axis=0)


def _route_tile(s1, s2):
    k_top = PEER_TOPK
    a = _top_values(s1, k_top)
    b = _top_values(s2, k_top)
    cv = _top_values(_candidate_sums(a, b), k_top)
    least = cv[k_top - 1:k_top, :]
    z = jnp.sum(jnp.exp(cv - cv[0:1, :]), axis=0, keepdims=True)
    rank2 = jnp.full_like(s2, float(k_top))
    for k in reversed(range(k_top)):
        rank2 = jnp.where(s2 >= b[k:k + 1, :], float(k), rank2)
    cnt = jnp.zeros_like(s1)
    for k in range(SUBLANES):
        cnt = jnp.where(s1 + b[k:k + 1, :] >= least, float(k + 1), cnt)
    tail = jnp.sum(jnp.where(a[0:1, :] + b[SUBLANES:, :] >= least, 1.0, 0.0), axis=0, keepdims=True)
    cnt = cnt + jnp.where(s1 == a[0:1, :], tail, 0.0)
    return rank2, jnp.exp(s2 - b[0:1, :]), cnt, jnp.exp(s1 - a[0:1, :]) * ((0.5 ** 0.5) / z)


def _score_weights_kernel(keys_ref, wq_ref, o_ref):
    o_ref[...] = lax.dot_general(keys_ref[0], wq_ref[...], _NT, precision=lax.Precision.HIGHEST,
                                 preferred_element_type=F32).astype(o_ref.dtype)


def _score_weights(keys, w_query):
    n_hp, n_keys, half = keys.shape
    d = w_query.shape[0]
    return pl.pallas_call(
        _score_weights_kernel,
        grid=(n_hp,),
        in_specs=[pl.BlockSpec((1, n_keys, half), lambda i: (i, 0, 0)),
                  pl.BlockSpec((d, half), lambda i: (0, i))],
        out_specs=pl.BlockSpec((n_keys, d), lambda i: (i, 0)),
        out_shape=jax.ShapeDtypeStruct((n_hp * n_keys, d), BF16),
        compiler_params=_compiler_params(("parallel",)),
        name="peer_score_weights",
    )(keys, w_query)


def _peer_route_kernel(x0_ref, xn_ref, w_ref, rk_ref, eb_ref, cnt_ref, ea_ref, s_ref):
    i = pl.program_id(0)

    def scores(x_ref, slot):
        s_ref[slot] = lax.dot_general(w_ref[...], x_ref[...].astype(BF16), _NT, preferred_element_type=F32)

    @pl.when(i == 0)
    def _():
        scores(x0_ref, 0)

    for cur in range(2):
        @pl.when(i % 2 == cur)
        def _():
            scores(xn_ref, 1 - cur)
            for h in range(PEER_HEADS):
                r1 = 2 * h * PEER_N_KEYS
                r2 = r1 + PEER_N_KEYS
                for lt in range(rk_ref.shape[2] // LANES):
                    sl = slice(lt * LANES, (lt + 1) * LANES)
                    rank2, eb, cnt, ea = _route_tile(s_ref[cur, r1:r2, sl], s_ref[cur, r2:r2 + PEER_N_KEYS, sl])
                    rk_ref[h, :, sl] = rank2.astype(BF16)
                    eb_ref[h, :, sl] = eb.astype(BF16)
                    cnt_ref[h, :, sl] = cnt
                    ea_ref[h, :, sl] = ea


def _peer_route(x1, w_scores, tr):
    n, d = x1.shape
    assert n % tr == 0 and tr % LANES == 0
    steps = n // tr
    out_block = pl.BlockSpec((PEER_HEADS, PEER_N_KEYS, tr), lambda i: (0, 0, i))
    shape = (PEER_HEADS, PEER_N_KEYS, n)
    return pl.pallas_call(
        _peer_route_kernel,
        grid=(steps,),
        in_specs=[pl.BlockSpec((tr, d), lambda i: (0, 0)),
                  pl.BlockSpec((tr, d), lambda i: (jnp.minimum(i + 1, steps - 1), 0)),
                  pl.BlockSpec(w_scores.shape, lambda i: (0, 0))],
        out_specs=[out_block] * 4,
        out_shape=[jax.ShapeDtypeStruct(shape, BF16), jax.ShapeDtypeStruct(shape, BF16),
                   jax.ShapeDtypeStruct(shape, F32), jax.ShapeDtypeStruct(shape, F32)],
        scratch_shapes=[pltpu.VMEM((2, w_scores.shape[0], tr), F32)],
        compiler_params=_compiler_params(("arbitrary",)),
        name="peer_route",
    )(x1, x1, w_scores)


def _peer_chunk(act_ref, i0, vt_ref, rk_ref, eb_ref, cnt_ref, ea_ref, rows_per_chunk):
    tm = act_ref.shape[1]
    tile = (BF16_TILE_ROWS, tm)
    tiles = (PEER_N_KEYS // BF16_TILE_ROWS, BF16_TILE_ROWS, tm)
    total = None
    for pair in range(rows_per_chunk // 2):
        rows = (2 * pair, 2 * pair + 1)
        gates = [None, None]
        for h in range(PEER_HEADS):
            rk = rk_ref[h].reshape(tiles)
            eb = eb_ref[h].reshape(tiles)
            for slot, ii in enumerate(rows):
                cnt_i = jnp.broadcast_to(cnt_ref[h, pl.ds(i0 + ii, 1), :], tile).astype(BF16)
                ea_i = jnp.broadcast_to(ea_ref[h, pl.ds(i0 + ii, 1), :], tile).astype(BF16)
                term = jnp.where(rk < cnt_i[None], ea_i[None] * eb, jnp.zeros_like(eb))
                gates[slot] = term if gates[slot] is None else gates[slot] + term
        coef = []
        for gate, ii in zip(gates, rows):
            a = act_ref[ii * PEER_N_KEYS:(ii + 1) * PEER_N_KEYS, :]
            gelu2 = a * (1.0 + lax.erf(a))
            coef.append(gate.reshape(PEER_N_KEYS, tm) * gelu2.astype(BF16))
        lo = 2 * pair * PEER_N_KEYS
        part = jnp.dot(vt_ref[0, :, lo:lo + 2 * PEER_N_KEYS], jnp.concatenate(coef, axis=0),
                       preferred_element_type=F32)
        total = part if total is None else total + part
    return total


def _peer_kernel(x_ref, u0_ref, un_ref, vt_ref, rk_ref, eb_ref, cnt_ref, ea_ref, g_ref, b_ref, o_ref,
                 xb_ref, act_ref, acc_ref, *, alpha, rows_per_chunk):
    c = pl.program_id(1)
    route = (rk_ref, eb_ref, cnt_ref, ea_ref)

    @pl.when(c == 0)
    def _():
        xb_ref[...] = x_ref[...].T.astype(BF16)
        act_ref[0] = jnp.dot(u0_ref[...], xb_ref[...], preferred_element_type=F32)
        acc_ref[...] = jnp.zeros_like(acc_ref)

    i0 = pl.multiple_of(c * rows_per_chunk, rows_per_chunk)
    for cur in range(2):
        @pl.when(c % 2 == cur)
        def _():
            act_ref[1 - cur] = jnp.dot(un_ref[...], xb_ref[...], preferred_element_type=F32)
            acc_ref[...] += _peer_chunk(act_ref.at[cur], i0, vt_ref, *route, rows_per_chunk)

    @pl.when(c == pl.num_programs(1) - 1)
    def _():
        peer = acc_ref[...].T
        o_ref[...] = _layer_norm(alpha * x_ref[...] + peer, g_ref[...], b_ref[...])


def _peer(x1, w_scores, u_bf, vt_bf, ln_g, ln_b, *, tr, tm, rows_per_chunk, alpha):
    n, d = x1.shape
    n_exp = u_bf.shape[0]
    chunk = rows_per_chunk * PEER_N_KEYS
    assert n % tm == 0 and n_exp % (2 * chunk) == 0 and rows_per_chunk % SUBLANES == 0
    n_chunks = n_exp // chunk
    routed = _peer_route(x1, w_scores, tr)
    const = lambda shape: pl.BlockSpec(shape, lambda i, c: (0,) * len(shape))
    route_block = pl.BlockSpec((PEER_HEADS, PEER_N_KEYS, tm), lambda i, c: (0, 0, i))
    return pl.pallas_call(
        functools.partial(_peer_kernel, alpha=alpha, rows_per_chunk=rows_per_chunk),
        grid=(n // tm, n_chunks),
        in_specs=[pl.BlockSpec((tm, d), lambda i, c: (i, 0)),
                  pl.BlockSpec((chunk, d), lambda i, c: (0, 0)),
                  pl.BlockSpec((chunk, d), lambda i, c: (jnp.minimum(c + 1, n_chunks - 1), 0)),
                  pl.BlockSpec((1, d, chunk), lambda i, c: (c, 0, 0)),
                  route_block, route_block, route_block, route_block,
                  const((1, d)), const((1, d))],
        out_specs=pl.BlockSpec((tm, d), lambda i, c: (i, 0)),
        out_shape=jax.ShapeDtypeStruct((n, d), F32),
        scratch_shapes=[pltpu.VMEM((d, tm), BF16), pltpu.VMEM((2, chunk, tm), F32), pltpu.VMEM((d, tm), F32)],
        compiler_params=_compiler_params(("parallel", "arbitrary")),
        name="peer_ln2",
    )(x1, u_bf, u_bf, vt_bf, *routed, ln_g, ln_b)


def _pick(n, prefs):
    for p in prefs:
        if n % p == 0:
            return p
    return n


def kernel(x_prompt, x_sample, cache_k, cache_v, state_pool, page_table, w_in, sb_bias, w_out, w_pool,
           pool_scale, ln1_g, ln1_b, w_query, sub_keys, u_table, v_table, ln2_g, ln2_b):
    depth = w_in.shape[0]
    b, s, d = x_prompt.shape
    db, t_new, _ = x_sample.shape
    n_pages = page_table.shape[1]
    past_len = n_pages * PAGE_SIZE
    alpha = (2.0 * depth) ** 0.25
    n_pool = cache_k.shape[1]

    hp, hs = x_prompt, x_sample
    outs = [[] for _ in range(6)]
    for l in range(depth):
        w_in_bf = w_in[l].astype(BF16)
        wout_bf = w_out[l].astype(BF16)
        wpool_bf = w_pool[l].astype(BF16)
        w_scores = _score_weights(sub_keys[l].reshape(2 * PEER_HEADS, PEER_N_KEYS, PEER_HALF), w_query[l])
        u_bf = (u_table[l] * (0.5 ** 0.5)).astype(BF16)
        vt_bf = v_table[l].reshape(-1, PEER_CHUNK_ROWS * PEER_N_KEYS, d).transpose(0, 2, 1).astype(BF16)
        pscale = pool_scale[l].reshape(1, POOL_WIDTH)
        g1, b1 = ln1_g[l].reshape(1, d), ln1_b[l].reshape(1, d)
        g2, b2 = ln2_g[l].reshape(1, d), ln2_b[l].reshape(1, d)
        bias = sb_bias[l].astype(F32)
        peer = functools.partial(_peer, w_scores=w_scores, u_bf=u_bf, vt_bf=vt_bf,
                                 ln_g=g2, ln_b=b2, rows_per_chunk=PEER_CHUNK_ROWS, alpha=alpha)

        n = b * s
        q, kt, vt, u, kb, vb = _in_proj(hp.reshape(n, d), w_in_bf, _pick(s, (512, 256, 128)), seq=s)
        attn = _sb_prompt(bias, q.reshape(b, s, SB_WIDTH), kb.reshape(b, s, SB_WIDTH),
                          vb.reshape(b, s, SB_WIDTH), _pick(s, (256, 128)))
        u3 = u.reshape(b, s, POOL_WIDTH)
        x1 = _mix(hp, attn, u3, u3, wpool_bf, pscale, wout_bf, g1, b1,
                  bb=1, t=_pick(s, (512, 256, 128)), alpha=alpha, pos0=0, zero_first_prev=True)
        hp = peer(x1.reshape(n, d), tr=_pick(n, (256, 128)), tm=_pick(n, (512, 256, 128))).reshape(b, s, d)
        outs[0].append(kt.reshape(b, SB_HEADS, SB_HEAD_DIM, s).transpose(0, 3, 1, 2))
        outs[1].append(vt.reshape(b, SB_HEADS, SB_HEAD_DIM, s).transpose(0, 3, 1, 2))
        outs[2].append(u3[:, s - POOL_STATE_LEN:])

        ns = db * t_new
        qs, ks, vs, us, _, _ = _in_proj(hs.reshape(ns, d), w_in_bf, _pick(ns, (512, 256, 128)))
        attn_s = _sb_sample(page_table, bias, qs.reshape(db, t_new, SB_WIDTH),
                            ks.reshape(db, t_new, SB_WIDTH), vs.reshape(db, t_new, SB_WIDTH),
                            cache_k[l].transpose(0, 2, 3, 1), cache_v[l].transpose(0, 2, 3, 1),
                            _pick(n_pages, (32, 16, 8, 4, 2, 1)))
        us3 = us.reshape(db, t_new, POOL_WIDTH)
        state = state_pool[l].astype(F32)
        prev = jnp.pad(state, ((0, 0), (POOL_PREV_ROWS - POOL_STATE_LEN, 0), (0, 0)))
        x1s = _mix(hs, attn_s, us3, prev, wpool_bf, pscale, wout_bf, g1, b1,
                   bb=_pick(db, (16, 8, 4, 2, 1)), t=t_new, alpha=alpha, pos0=past_len, zero_first_prev=False)
        hs = peer(x1s.reshape(ns, d), tr=_pick(ns, (256, 128)), tm=_pick(ns, (512, 256, 128))).reshape(db, t_new, d)
        outs[3].append(ks.reshape(db, t_new, SB_HEADS, SB_HEAD_DIM))
        outs[4].append(vs.reshape(db, t_new, SB_HEADS, SB_HEAD_DIM))
        outs[5].append(jnp.concatenate([state, us3], axis=1)[:, -POOL_STATE_LEN:])

    return (hp, hs, jnp.stack(outs[0]), jnp.stack(outs[1]), jnp.stack(outs[2]),
            jnp.stack(outs[3]), jnp.stack(outs[4]), jnp.stack(outs[5]))
```
